```python
import math
import jax, jax.numpy as jnp
from jax import lax
import numpy as np

D_MODEL = 1024
BATCH = 16
SEQ = 4096
DEPTH = 2
DEC_BATCH = 16
DEC_SEQ = 32
PAST_LEN = 1024

CHUNK = 64
HEAD_DIM = 64
DN_HEADS = 6
DN_WIDTH = DN_HEADS * HEAD_DIM
DN_CONV = 4
DN_BLOCK = CHUNK
AT_HEADS = 6
AT_WIDTH = AT_HEADS * HEAD_DIM
BAND_CHUNKS = 8
BAND_ROWS = BAND_CHUNKS * CHUNK
REL_CLIP = 128
POOL_WINDOWS = (2, 4, 8, 16)
POOL_GROUPS = len(POOL_WINDOWS)
POOL_GROUP_DIM = 64
POOL_WIDTH = POOL_GROUPS * POOL_GROUP_DIM
POOL_HIST = max(POOL_WINDOWS) - 1
MIX_WIDTH = DN_WIDTH + AT_WIDTH + POOL_WIDTH
N_EXPERTS = 16
N_EXPERT_GROUPS = 4
EXPERTS_PER_GROUP = N_EXPERTS // N_EXPERT_GROUPS
TOP_K = 2
D_EXPERT = 256
DEEPNORM_ALPHA = (2 * DEPTH) ** 0.25
DEEPNORM_BETA = (8 * DEPTH) ** -0.25
LN_EPS = 1e-5
RMS_EPS = 1e-6
OFF_AQ = 0
OFF_AK = OFF_AQ + DN_WIDTH
OFF_AV = OFF_AK + DN_WIDTH
OFF_AZ = OFF_AV + DN_WIDTH
OFF_AA = OFF_AZ + DN_WIDTH
OFF_AB = OFF_AA + DN_HEADS
OFF_BQ = OFF_AB + DN_HEADS
OFF_BK = OFF_BQ + AT_WIDTH
OFF_BV = OFF_BK + AT_WIDTH
OFF_CU = OFF_BV + AT_WIDTH
IN_WIDTH = OFF_CU + POOL_WIDTH

kernel_name = 'hybrid_streaming_encoder_step'


def layer_norm(x, g, b):
    xf = x.astype(jnp.float32)
    mu = xf.mean(-1, keepdims=True)
    var = jnp.mean(jnp.square(xf - mu), -1, keepdims=True)
    return ((xf - mu) * lax.rsqrt(var + LN_EPS) * g.astype(jnp.float32) + b.astype(jnp.float32)).astype(x.dtype)


def rms_norm(x):
    return x * lax.rsqrt(jnp.mean(jnp.square(x), -1, keepdims=True) + RMS_EPS)


def l2_normalize(x):
    return x * lax.rsqrt(jnp.sum(jnp.square(x), -1, keepdims=True) + RMS_EPS)


def last_rows(t, n):
    if t.shape[1] < n:
        t = jnp.concatenate([jnp.zeros((t.shape[0], n - t.shape[1]) + t.shape[2:], t.dtype), t], axis=1)
    return t[:, t.shape[1] - n:]


def causal_dwconv(u, hist, w):
    L = u.shape[1]
    ext = jnp.concatenate([hist.astype(u.dtype), u], axis=1)
    out = sum(ext[:, j:j + L] * w[j] for j in range(DN_CONV))
    return out, ext[:, -(DN_CONV - 1):]


def _to_blocks(t, n, c):
    B, L, H = t.shape[:3]
    t = t.reshape((B, n, c, H) + t.shape[3:])
    return jnp.swapaxes(jnp.moveaxis(t, 1, 0), 2, 3)


def gated_delta_rule(q, k, v, g, beta, s0):
    B, L, H, dk = q.shape
    dv = v.shape[-1]
    c = min(DN_BLOCK, L)
    n = L // c
    qb, kb, vb = _to_blocks(q, n, c), _to_blocks(k, n, c), _to_blocks(v, n, c)
    gb, bb = _to_blocks(g, n, c), _to_blocks(beta, n, c)
    gc = jnp.cumsum(gb, axis=-1)
    incl = jnp.tril(jnp.ones((c, c), bool))
    strict = jnp.tril(jnp.ones((c, c), bool), -1)
    diff = gc[..., :, None] - gc[..., None, :]
    decay = jnp.where(incl, jnp.exp(jnp.where(incl, diff, 0.0)), 0.0)
    kbeta = kb * bb[..., None]
    m = jnp.where(strict, jnp.einsum('nbhid,nbhjd->nbhij', kbeta, kb) * decay, 0.0)
    rhs = jnp.concatenate([vb * bb[..., None], kbeta * jnp.exp(gc)[..., None]], axis=-1)
    sol = lax.linalg.triangular_solve(jnp.eye(c, dtype=q.dtype) + m, rhs,
                                      left_side=True, lower=True, unit_diagonal=True)
    u, w = sol[..., :dv], sol[..., dv:]
    a_intra = jnp.where(incl, jnp.einsum('nbhid,nbhjd->nbhij', qb, kb) * decay, 0.0)
    q_dec = qb * jnp.exp(gc)[..., None]
    g_last = gc[..., -1]
    k_tail = kb * jnp.exp(g_last[..., None] - gc)[..., None]

    def step(s, xs):
        qd_i, a_i, u_i, w_i, kt_i, gl_i = xs
        v_new = u_i - jnp.einsum('bhcd,bhde->bhce', w_i, s)
        o = jnp.einsum('bhcd,bhde->bhce', qd_i, s) + jnp.einsum('bhij,bhje->bhie', a_i, v_new)
        s = s * jnp.exp(gl_i)[..., None, None] + jnp.einsum('bhcd,bhce->bhde', kt_i, v_new)
        return s, o

    s_final, o = lax.scan(step, s0, (q_dec, a_intra, u, w, k_tail, g_last))
    o = jnp.moveaxis(jnp.swapaxes(o, 2, 3), 0, 1).reshape(B, L, H, dv)
    return o, s_final


def rel_bias_lookup(rel_table, dist):
    return rel_table[:, jnp.clip(dist, -REL_CLIP, REL_CLIP) + REL_CLIP].astype(jnp.float32)


def band_softmax(q, k, v, bias, valid):
    s = jnp.einsum('...ihd,...jhd->...hij', q, k).astype(jnp.float32) * HEAD_DIM ** -0.5 + bias
    if valid is not None:
        s = jnp.where(valid[..., None, None, :], s, -1e30)
    p = jax.nn.softmax(s, axis=-1).astype(v.dtype)
    return jnp.einsum('...hij,...jhd->...ihd', p, v)


def band_attn_prompt(q, k, v, rel_table):
    B, L, H, d = q.shape
    n = L // CHUNK
    nb = BAND_CHUNKS + 1
    a = jnp.arange(CHUNK)
    o = jnp.arange(nb)
    dist = (BAND_CHUNKS - o)[:, None, None] * CHUNK + a[None, :, None] - a[None, None, :]
    dist = jnp.transpose(dist, (1, 0, 2)).reshape(CHUNK, nb * CHUNK)
    bias = rel_bias_lookup(rel_table, dist)
    valid = jnp.repeat((jnp.arange(n)[:, None] - BAND_CHUNKS + o[None, :]) >= 0, CHUNK, axis=1)

    def one_seq(args):
        qs, ks, vs = args
        pad = jnp.zeros((BAND_ROWS, H, d), ks.dtype)
        kc = jnp.concatenate([pad, ks]).reshape(n + BAND_CHUNKS, CHUNK, H, d)
        vc = jnp.concatenate([pad, vs]).reshape(n + BAND_CHUNKS, CHUNK, H, d)
        kband = jnp.stack([kc[j:j + n] for j in range(nb)], axis=1).reshape(n, nb * CHUNK, H, d)
        vband = jnp.stack([vc[j:j + n] for j in range(nb)], axis=1).reshape(n, nb * CHUNK, H, d)
        out = band_softmax(qs.reshape(n, CHUNK, H, d), kband, vband, bias, valid)
        return out.reshape(L, H, d)

    return lax.map(one_seq, (q, k, v))


def band_attn_sample(q, k, v, k_hist, v_hist, rel_table):
    Lq = q.shape[1]
    W = k_hist.shape[1]
    kk = jnp.concatenate([k_hist.astype(k.dtype), k], axis=1)
    vv = jnp.concatenate([v_hist.astype(v.dtype), v], axis=1)
    key_off = jnp.concatenate([jnp.arange(W) - W, jnp.arange(Lq)])
    bias = rel_bias_lookup(rel_table, jnp.arange(Lq)[:, None] - key_off[None, :])
    out = band_softmax(q, kk, vv, bias, None)
    return out, last_rows(kk, W), last_rows(vv, W)


def multiscale_pool(u, hist, start, pool_w, pool_scale):
    B, L, _ = u.shape
    ext = jnp.concatenate([hist.astype(u.dtype), u], axis=1)
    cs = jnp.cumsum(jnp.pad(ext.astype(jnp.float32), ((0, 0), (1, 0), (0, 0))), axis=1)
    pos = (start + jnp.arange(L)).astype(jnp.float32)
    uf = u.astype(jnp.float32)
    res = []
    for gi, w in enumerate(POOL_WINDOWS):
        c0, c1 = gi * POOL_GROUP_DIM, (gi + 1) * POOL_GROUP_DIM
        wsum = cs[:, POOL_HIST + 1:POOL_HIST + 1 + L, c0:c1] - cs[:, POOL_HIST + 1 - w:POOL_HIST + 1 - w + L, c0:c1]
        cnt = jnp.minimum(pos + 1.0, float(w))[None, :, None]
        res.append(wsum / cnt - uf[..., c0:c1])
    r = jnp.stack(res, axis=2)
    y = jnp.einsum('blgc,gcd->blgd', r, pool_w.astype(jnp.float32)).reshape(B, L, POOL_WIDTH)
    y = y * pool_scale.astype(jnp.float32)
    return y.astype(u.dtype), ext[:, -POOL_HIST:]


def routed_ffn(x, w_router, router_bias, w_gate, w_up, w_down):
    B, L, D = x.shape
    xt = x.reshape(B * L, D)
    affinity = jax.nn.sigmoid((xt @ w_router).astype(jnp.float32))
    sel = affinity + router_bias.astype(jnp.float32)
    grp_score = lax.top_k(sel.reshape(-1, N_EXPERT_GROUPS, EXPERTS_PER_GROUP), TOP_K)[0].sum(-1)
    best_grp = jnp.argmax(grp_score, axis=-1)
    in_grp = (jnp.arange(N_EXPERTS) // EXPERTS_PER_GROUP)[None, :] == best_grp[:, None]
    _, idx = lax.top_k(jnp.where(in_grp, sel, -jnp.inf), TOP_K)
    wts = jnp.take_along_axis(affinity, idx, axis=-1)
    wts = wts / wts.sum(-1, keepdims=True)
    gates = jnp.einsum('tk,tke->te', wts, jax.nn.one_hot(idx, N_EXPERTS, dtype=jnp.float32)).astype(x.dtype)
    out = jnp.zeros_like(xt)
    for e in range(N_EXPERTS):
        hid = jax.nn.silu(xt @ w_gate[e]) * (xt @ w_up[e])
        out = out + (hid @ w_down[e]) * gates[:, e:e + 1]
    return out.reshape(B, L, D)


def trunk(x, start, dn_state, conv_state, k_hist, v_hist, pool_state, band_rows, weights):
    (w_in, w_conv, a_log, dt_bias, dn_norm_g, rel_table, pool_w, pool_scale, w_out,
     ln1_g, ln1_b, w_router, router_bias, w_gate, w_up, w_down, ln2_g, ln2_b) = weights
    f32 = jnp.float32
    B, L, _ = x.shape
    new_dn, new_conv, new_k, new_v, new_pool = [], [], [], [], []
    for l in range(DEPTH):
        h = x @ w_in[l]
        qkv, conv_new = causal_dwconv(h[..., OFF_AQ:OFF_AZ], conv_state[l], w_conv[l])
        qkv = jax.nn.silu(qkv.astype(f32)).reshape(B, L, 3, DN_HEADS, HEAD_DIM)
        qa = l2_normalize(qkv[:, :, 0]) * HEAD_DIM ** -0.5
        ka = l2_normalize(qkv[:, :, 1])
        va = qkv[:, :, 2]
        log_decay = -jnp.exp(a_log[l].astype(f32)) * jax.nn.softplus(h[..., OFF_AA:OFF_AB].astype(f32) + dt_bias[l].astype(f32))
        beta = jax.nn.sigmoid(h[..., OFF_AB:OFF_BQ].astype(f32))
        oa, dn_new = gated_delta_rule(qa, ka, va, log_decay, beta, dn_state[l].astype(f32))
        z = h[..., OFF_AZ:OFF_AA].astype(f32).reshape(B, L, DN_HEADS, HEAD_DIM)
        oa = (rms_norm(oa) * dn_norm_g[l].astype(f32) * jax.nn.silu(z)).reshape(B, L, DN_WIDTH).astype(x.dtype)
        qb = h[..., OFF_BQ:OFF_BK].reshape(B, L, AT_HEADS, HEAD_DIM)
        kb = h[..., OFF_BK:OFF_BV].reshape(B, L, AT_HEADS, HEAD_DIM)
        vb = h[..., OFF_BV:OFF_CU].reshape(B, L, AT_HEADS, HEAD_DIM)
        if k_hist is None:
            ob = band_attn_prompt(qb, kb, vb, rel_table[l])
            k_new, v_new = last_rows(kb, band_rows), last_rows(vb, band_rows)
        else:
            ob, k_new, v_new = band_attn_sample(qb, kb, vb, k_hist[l], v_hist[l], rel_table[l])
        ob = ob.reshape(B, L, AT_WIDTH)
        oc, pool_new = multiscale_pool(h[..., OFF_CU:IN_WIDTH], pool_state[l], start, pool_w[l], pool_scale[l])
        mix = jnp.concatenate([oa, ob, oc], axis=-1) @ w_out[l]
        x = layer_norm(DEEPNORM_ALPHA * x + mix, ln1_g[l], ln1_b[l])
        ffn = routed_ffn(x, w_router, router_bias, w_gate[l], w_up[l], w_down[l])
        x = layer_norm(DEEPNORM_ALPHA * x + ffn, ln2_g[l], ln2_b[l])
        new_dn.append(dn_new.astype(dn_state.dtype))
        new_conv.append(conv_new)
        new_k.append(k_new)
        new_v.append(v_new)
        new_pool.append(pool_new)
    return (x, jnp.stack(new_dn), jnp.stack(new_conv), jnp.stack(new_k), jnp.stack(new_v), jnp.stack(new_pool))


def setup_inputs(seed: int = 0) -> dict:
    key = jax.random.key(seed)
    ks = jax.random.split(key, 32)
    f32 = jnp.float32

    def nrm(k, shape, scale):
        return jax.random.normal(k, shape, f32) * scale

    band = min(BAND_ROWS, PAST_LEN)
    dt = jnp.exp(jax.random.uniform(ks[10], (DEPTH, DN_HEADS), f32, math.log(1e-3), math.log(1e-1)))
    return {
        'x_prompt': nrm(ks[0], (BATCH, SEQ, D_MODEL), 1.0),
        'x_sample': nrm(ks[1], (DEC_BATCH, DEC_SEQ, D_MODEL), 1.0),
        'state_dn': nrm(ks[2], (DEPTH, DEC_BATCH, DN_HEADS, HEAD_DIM, HEAD_DIM), 0.1),
        'state_conv': nrm(ks[3], (DEPTH, DEC_BATCH, DN_CONV - 1, 3 * DN_WIDTH), 1.0),
        'cache_k': nrm(ks[4], (DEPTH, DEC_BATCH, band, AT_HEADS, HEAD_DIM), 1.0),
        'cache_v': nrm(ks[5], (DEPTH, DEC_BATCH, band, AT_HEADS, HEAD_DIM), 1.0),
        'state_pool': nrm(ks[6], (DEPTH, DEC_BATCH, POOL_HIST, POOL_WIDTH), 1.0),
        'w_in': nrm(ks[7], (DEPTH, D_MODEL, IN_WIDTH), D_MODEL ** -0.5),
        'w_conv': nrm(ks[8], (DEPTH, DN_CONV, 3 * DN_WIDTH), DN_CONV ** -0.5),
        'a_log': jnp.log(jax.random.uniform(ks[9], (DEPTH, DN_HEADS), f32, 1.0, 16.0)),
        'dt_bias': dt + jnp.log(-jnp.expm1(-dt)),
        'dn_norm_g': 1.0 + nrm(ks[11], (DEPTH, HEAD_DIM), 0.02),
        'rel_table': nrm(ks[12], (DEPTH, AT_HEADS, 2 * REL_CLIP + 1), 0.2),
        'pool_w': nrm(ks[13], (DEPTH, POOL_GROUPS, POOL_GROUP_DIM, POOL_GROUP_DIM), POOL_GROUP_DIM ** -0.5),
        'pool_scale': 1.0 + nrm(ks[14], (DEPTH, POOL_WIDTH), 0.02),
        'w_out': nrm(ks[15], (DEPTH, MIX_WIDTH, D_MODEL), MIX_WIDTH ** -0.5 * DEEPNORM_BETA),
        'ln1_g': 1.0 + nrm(ks[16], (DEPTH, D_MODEL), 0.02),
        'ln1_b': nrm(ks[17], (DEPTH, D_MODEL), 0.02),
        'w_router': nrm(ks[18], (D_MODEL, N_EXPERTS), D_MODEL ** -0.5),
        'router_bias': nrm(ks[19], (N_EXPERTS,), 0.01),
        'w_gate': nrm(ks[20], (DEPTH, N_EXPERTS, D_MODEL, D_EXPERT), D_MODEL ** -0.5),
        'w_up': nrm(ks[21], (DEPTH, N_EXPERTS, D_MODEL, D_EXPERT), D_MODEL ** -0.5),
        'w_down': nrm(ks[22], (DEPTH, N_EXPERTS, D_EXPERT, D_MODEL), D_EXPERT ** -0.5 * DEEPNORM_BETA),
        'ln2_g': 1.0 + nrm(ks[23], (DEPTH, D_MODEL), 0.02),
        'ln2_b': nrm(ks[24], (DEPTH, D_MODEL), 0.02),
    }


def reference(x_prompt, x_sample, state_dn, state_conv, cache_k, cache_v, state_pool,
              w_in, w_conv, a_log, dt_bias, dn_norm_g, rel_table, pool_w, pool_scale, w_out,
              ln1_g, ln1_b, w_router, router_bias, w_gate, w_up, w_down, ln2_g, ln2_b):
    weights = (w_in, w_conv, a_log, dt_bias, dn_norm_g, rel_table, pool_w, pool_scale, w_out,
               ln1_g, ln1_b, w_router, router_bias, w_gate, w_up, w_down, ln2_g, ln2_b)
    band_rows = cache_k.shape[2]
    bp = x_prompt.shape[0]
    zero_dn = jnp.zeros((DEPTH, bp, DN_HEADS, HEAD_DIM, HEAD_DIM), state_dn.dtype)
    zero_conv = jnp.zeros((DEPTH, bp, DN_CONV - 1, 3 * DN_WIDTH), x_prompt.dtype)
    zero_pool = jnp.zeros((DEPTH, bp, POOL_HIST, POOL_WIDTH), x_prompt.dtype)
    y_prompt, p_dn, p_conv, p_k, p_v, p_pool = trunk(
        x_prompt, 0, zero_dn, zero_conv, None, None, zero_pool, band_rows, weights)
    y_sample, s_dn, s_conv, s_k, s_v, s_pool = trunk(
        x_sample, PAST_LEN, state_dn, state_conv, cache_k, cache_v, state_pool, band_rows, weights)
    return (y_prompt, y_sample, p_dn, p_conv, p_k, p_v, p_pool, s_dn, s_conv, s_k, s_v, s_pool)
```

```python
import functools
import math

import jax
import jax.numpy as jnp
from jax import lax
from jax.experimental import pallas as pl
from jax.experimental.pallas import tpu as pltpu

F32 = jnp.float32
BF16 = jnp.bfloat16
HIGHEST = lax.Precision.HIGHEST

D_MODEL = 1024
HEAD_DIM = 64
N_HEADS = 6
HEADS_WIDTH = N_HEADS * HEAD_DIM
QKV_WIDTH = 3 * HEADS_WIDTH
DN_CONV = 4
CHUNK = 64
BAND_ROWS = 512
REL_CLIP = 128
POOL_WINDOWS = (2, 4, 8, 16)
POOL_GROUP_DIM = 64
POOL_WIDTH = 256
POOL_HIST = 15
N_EXPERTS = 16
N_EXPERT_GROUPS = 4
EXPERTS_PER_GROUP = 4
D_EXPERT = 256
DEPTH = 2
DEEPNORM_ALPHA = (2 * DEPTH) ** 0.25
LN_EPS = 1e-5
RMS_EPS = 1e-6
PAST_LEN = 1024

OFF_AQ = 0
OFF_AZ = 3 * HEADS_WIDTH
OFF_AA = OFF_AZ + HEADS_WIDTH
OFF_AB = OFF_AA + N_HEADS
OFF_BQ = OFF_AB + N_HEADS
OFF_CU = OFF_BQ + 3 * HEADS_WIDTH
IN_WIDTH = OFF_CU + POOL_WIDTH

LANES = 128
GATE_WIDTH = LANES
SEG_A = (0, QKV_WIDTH)
SEG_Z = (SEG_A[1], SEG_A[1] + HEADS_WIDTH)
SEG_B = (SEG_Z[1], SEG_Z[1] + QKV_WIDTH)
SEG_C = (SEG_B[1], SEG_B[1] + POOL_WIDTH)
SEG_G = (SEG_C[1], SEG_C[1] + GATE_WIDTH)
PERM_WIDTH = SEG_G[1]

ROW_TILE = 512
VMEM_LIMIT = 56 * 1024 * 1024


def _params(*sem):
    return pltpu.CompilerParams(dimension_semantics=sem, vmem_limit_bytes=VMEM_LIMIT)


def _dot(a, b, precision=None):
    return jnp.dot(a, b, preferred_element_type=F32, precision=precision)


def _dot_nt(a, b, precision=None):
    return lax.dot_general(a, b, (((1,), (1,)), ((), ())), preferred_element_type=F32, precision=precision)


def _dot_tn(a, b, precision=None):
    return lax.dot_general(a, b, (((0,), (0,)), ((), ())), preferred_element_type=F32, precision=precision)


def _silu(x):
    return x * jax.nn.sigmoid(x)


def _layer_norm(x, g, b):
    mu = jnp.mean(x, axis=-1, keepdims=True)
    xc = x - mu
    var = jnp.mean(xc * xc, axis=-1, keepdims=True)
    return xc * lax.rsqrt(var + LN_EPS) * g + b


def _inproj_kernel(full_f32, x_ref, w_ref, ha_ref, hz_ref, hb_ref, hc_ref, hg_ref):
    prec = HIGHEST if full_f32 else None
    xb = x_ref[...].astype(w_ref.dtype)
    for (lo, hi), out_ref in ((SEG_A, ha_ref), (SEG_Z, hz_ref), (SEG_B, hb_ref), (SEG_C, hc_ref), (SEG_G, hg_ref)):
        step = 384 if (hi - lo) % 384 == 0 else hi - lo
        for c0 in range(0, hi - lo, step):
            out_ref[:, c0:c0 + step] = _dot(xb, w_ref[:, lo + c0:lo + c0 + step], prec)


def _inproj(x2d, w_perm, full_f32):
    t = x2d.shape[0]
    tm = min(ROW_TILE, t)
    widths = [s[1] - s[0] for s in (SEG_A, SEG_Z, SEG_B, SEG_C, SEG_G)]
    return pl.pallas_call(
        functools.partial(_inproj_kernel, full_f32),
        grid=(t // tm,),
        in_specs=[pl.BlockSpec((tm, D_MODEL), lambda i: (i, 0)),
                  pl.BlockSpec((D_MODEL, PERM_WIDTH), lambda i: (0, 0))],
        out_specs=[pl.BlockSpec((tm, w), lambda i: (i, 0)) for w in widths],
        out_shape=[jax.ShapeDtypeStruct((t, w), F32) for w in widths],
        compiler_params=_params("arbitrary"),
        name="inproj",
    )(x2d, w_perm)


def _delta_kernel(c, full_f32, ha_ref, hz_ref, hg_ref, conv0_ref, dn0_ref, wconv_ref, alog_ref, dtb_ref, ng_ref,
                  oa_ref, dn_ref, s_ref, ext_ref):
    n = pl.program_id(1)
    prec = HIGHEST if full_f32 else None
    pad = 8 - (DN_CONV - 1)

    @pl.when(n == 0)
    def _():
        s_ref[...] = dn0_ref[...]
        ext_ref[pad:8, :] = conv0_ref[...]

    u = ha_ref[...]
    ext_ref[8:8 + c, :] = u
    conv = ext_ref[pad:pad + c, :] * wconv_ref[0:1, :]
    for j in range(1, DN_CONV):
        conv = conv + ext_ref[pad + j:pad + j + c, :] * wconv_ref[j:j + 1, :]
    ext_ref[pad:8, :] = u[c - (DN_CONV - 1):c, :]
    qkv = _silu(conv)
    q = qkv[:, 0:HEADS_WIDTH]
    k = qkv[:, HEADS_WIDTH:2 * HEADS_WIDTH]
    v = qkv[:, 2 * HEADS_WIDTH:QKV_WIDTH]

    r = lax.broadcasted_iota(jnp.int32, (HEADS_WIDTH, HEADS_WIDTH), 0) // HEAD_DIM
    cc = lax.broadcasted_iota(jnp.int32, (HEADS_WIDTH, HEADS_WIDTH), 1) // HEAD_DIM
    head_ones = (r == cc).astype(F32)
    qn = q * lax.rsqrt(_dot(q * q, head_ones, HIGHEST) + RMS_EPS) * HEAD_DIM ** -0.5
    kn = k * lax.rsqrt(_dot(k * k, head_ones, HIGHEST) + RMS_EPS)

    hg = hg_ref[...]
    lane = lax.broadcasted_iota(jnp.int32, (1, GATE_WIDTH), 1)
    neg_rate = jnp.where(lane < N_HEADS, -jnp.exp(alog_ref[...]), 0.0)
    g = neg_rate * jax.nn.softplus(hg + dtb_ref[...])
    beta = jax.nn.sigmoid(hg)
    ri = lax.broadcasted_iota(jnp.int32, (c, c), 0)
    ci = lax.broadcasted_iota(jnp.int32, (c, c), 1)
    incl = ri >= ci
    strict = ri > ci
    gc = _dot(incl.astype(F32), g, HIGHEST)
    gc_t = _dot_tn(g, (ri <= ci).astype(F32), HIGHEST)

    z = hz_ref[...]
    for h in range(N_HEADS):
        sl = slice(h * HEAD_DIM, (h + 1) * HEAD_DIM)
        qh, kh, vh = qn[:, sl], kn[:, sl], v[:, sl]
        gch = gc[:, h:h + 1]
        diff = gch - gc_t[h:h + 1, :]
        decay = jnp.where(incl, jnp.exp(jnp.where(incl, diff, 0.0)), 0.0)
        bh = beta[:, N_HEADS + h:N_HEADS + h + 1]
        kbeta = kh * bh
        eg = jnp.exp(gch)
        neg_m = jnp.where(strict, -(_dot_nt(kbeta, kh, prec) * decay), 0.0)
        sol = jnp.concatenate([vh * bh, kbeta * eg], axis=-1)
        power = neg_m
        sol = sol + _dot(power, sol, prec)
        for _ in range(int(math.log2(c)) - 1):
            power = _dot(power, power, prec)
            sol = sol + _dot(power, sol, prec)
        u_h, w_h = sol[:, 0:HEAD_DIM], sol[:, HEAD_DIM:2 * HEAD_DIM]
        a_intra = jnp.where(incl, _dot_nt(qh, kh, prec) * decay, 0.0)
        g_last = gc[c - 1:c, h:h + 1]
        k_tail = kh * jnp.exp(g_last - gch)
        s = s_ref[h]
        v_new = u_h - _dot(w_h, s, prec)
        o = _dot(qh * eg, s, prec) + _dot(a_intra, v_new, prec)
        s_ref[h] = s * jnp.exp(g_last) + _dot_tn(k_tail, v_new, prec)
        o = o * lax.rsqrt(jnp.mean(o * o, axis=-1, keepdims=True) + RMS_EPS)
        oa_ref[:, sl] = o * ng_ref[:, sl] * _silu(z[:, sl])

    @pl.when(n == pl.num_programs(1) - 1)
    def _():
        dn_ref[...] = s_ref[...]


def _delta(ha, hz, hg, conv0, dn0, w_conv, a_log, dt_bias, norm_g, full_f32):
    b, l, _ = ha.shape
    c = min(CHUNK, l)
    row = lambda w: pl.BlockSpec((None, c, w), lambda i, n: (i, n, 0))
    whole = lambda shape: pl.BlockSpec(shape, lambda i, n: (0,) * len(shape))
    state = pl.BlockSpec((None, N_HEADS, HEAD_DIM, HEAD_DIM), lambda i, n: (i, 0, 0, 0))
    return pl.pallas_call(
        functools.partial(_delta_kernel, c, full_f32),
        grid=(b, l // c),
        in_specs=[row(QKV_WIDTH), row(HEADS_WIDTH), row(GATE_WIDTH),
                  pl.BlockSpec((None, DN_CONV - 1, QKV_WIDTH), lambda i, n: (i, 0, 0)), state,
                  whole((DN_CONV, QKV_WIDTH)), whole((1, GATE_WIDTH)), whole((1, GATE_WIDTH)),
                  whole((1, HEADS_WIDTH))],
        out_specs=[row(HEADS_WIDTH), state],
        out_shape=[jax.ShapeDtypeStruct((b, l, HEADS_WIDTH), F32),
                   jax.ShapeDtypeStruct((b, N_HEADS, HEAD_DIM, HEAD_DIM), F32)],
        scratch_shapes=[pltpu.VMEM((N_HEADS, HEAD_DIM, HEAD_DIM), F32),
                        pltpu.VMEM((8 + c, QKV_WIDTH), F32)],
        compiler_params=_params("arbitrary", "arbitrary"),
        name="delta_rule",
    )(ha, hz, hg, conv0, dn0, w_conv, a_log, dt_bias, norm_g)


def _attn_kernel(c, tq, mask_before_start, full_f32, q_ref, kp_ref, ks_ref, vp_ref, vs_ref, bias_ref, o_ref,
                 kwin_ref, vwin_ref):
    i = pl.program_id(1)
    prec = HIGHEST if full_f32 else None
    w = BAND_ROWS
    kwin_ref[0:w, :] = kp_ref[...]
    kwin_ref[w:w + tq, :] = ks_ref[...]
    vwin_ref[0:w, :] = vp_ref[...]
    vwin_ref[w:w + tq, :] = vs_ref[...]
    key_idx = lax.broadcasted_iota(jnp.int32, (1, w + c), 1)

    def chunk(j, carry):
        r0 = pl.multiple_of(j * c, c)
        q = q_ref[pl.ds(r0, c), :]
        kk = kwin_ref[pl.ds(r0, w + c), :]
        vv = vwin_ref[pl.ds(r0, w + c), :]
        valid = (i * tq + r0 + key_idx - w) >= 0
        for h in range(N_HEADS):
            sl = slice(h * HEAD_DIM, (h + 1) * HEAD_DIM)
            s = _dot_nt(q[:, sl], kk[:, sl], prec) * HEAD_DIM ** -0.5 + bias_ref[h]
            if mask_before_start:
                s = jnp.where(valid, s, -1e30)
            s = s - jnp.max(s, axis=-1, keepdims=True)
            p = jnp.exp(s)
            p = p / jnp.sum(p, axis=-1, keepdims=True)
            o_ref[pl.ds(r0, c), sl] = _dot(p, vv[:, sl], prec)
        return carry

    lax.fori_loop(0, tq // c, chunk, 0)


def _attention(q, k, v, k_hist, v_hist, bias, full_f32):
    b, l, _ = q.shape
    c = min(CHUNK, l)
    tq = min(ROW_TILE, l)
    assert tq == BAND_ROWS or l == tq
    own = pl.BlockSpec((None, tq, HEADS_WIDTH), lambda bi, i: (bi, i, 0))
    if k_hist is None:
        prev = pl.BlockSpec((None, BAND_ROWS, HEADS_WIDTH), lambda bi, i: (bi, jnp.maximum(i - 1, 0), 0))
        k_prev, v_prev = k, v
    else:
        prev = pl.BlockSpec((None, BAND_ROWS, HEADS_WIDTH), lambda bi, i: (bi, 0, 0))
        k_prev, v_prev = k_hist, v_hist
    return pl.pallas_call(
        functools.partial(_attn_kernel, c, tq, k_hist is None, full_f32),
        grid=(b, l // tq),
        in_specs=[own, prev, own, prev, own,
                  pl.BlockSpec((N_HEADS, c, BAND_ROWS + c), lambda bi, i: (0, 0, 0))],
        out_specs=own,
        out_shape=jax.ShapeDtypeStruct((b, l, HEADS_WIDTH), F32),
        scratch_shapes=[pltpu.VMEM((BAND_ROWS + tq, HEADS_WIDTH), F32),
                        pltpu.VMEM((BAND_ROWS + tq, HEADS_WIDTH), F32)],
        compiler_params=_params("arbitrary", "arbitrary"),
        name="band_attention",
    )(q, k_prev, k, v_prev, v, bias)


def _mix_kernel(tile, start, full_f32, oa_ref, ob_ref, hc_ref, x_ref, pool0_ref, pw_ref, ps_ref, wout_ref, g_ref, b_ref,
                x1_ref, ext_ref):
    i = pl.program_id(1)
    prec = HIGHEST if full_f32 else None
    wdt = wout_ref.dtype
    hist0 = 16 - POOL_HIST

    @pl.when(i == 0)
    def _():
        ext_ref[0:hist0, :] = jnp.zeros((hist0, POOL_WIDTH), F32)
        ext_ref[hist0:16, :] = pool0_ref[...]

    u = hc_ref[...]
    ext_ref[16:16 + tile, :] = u
    window = 2 << (lax.broadcasted_iota(jnp.int32, (1, POOL_WIDTH), 1) // POOL_GROUP_DIM)
    wsum = u
    for j in range(1, max(POOL_WINDOWS)):
        wsum = wsum + jnp.where(j < window, ext_ref[16 - j:16 - j + tile, :], 0.0)
    ext_ref[hist0:16, :] = u[tile - POOL_HIST:tile, :]
    pos1 = start + i * tile + 1 + lax.broadcasted_iota(jnp.int32, (tile, 1), 0)
    cnt = jnp.minimum(pos1, window).astype(F32)
    res = wsum / cnt - u
    oc = _dot(res, pw_ref[...], prec) * ps_ref[...]

    mix = _dot(oa_ref[...].astype(wdt), wout_ref[0:HEADS_WIDTH, :], prec)
    mix = mix + _dot(ob_ref[...].astype(wdt), wout_ref[HEADS_WIDTH:2 * HEADS_WIDTH, :], prec)
    mix = mix + _dot(oc.astype(wdt), wout_ref[2 * HEADS_WIDTH:D_MODEL, :], prec)
    x1_ref[...] = _layer_norm(DEEPNORM_ALPHA * x_ref[...] + mix, g_ref[...], b_ref[...])


def _mix(oa, ob, hc, x, pool0, pool_wbd, pool_scale, w_out, ln_g, ln_b, start, full_f32):
    b, l, _ = x.shape
    tile = min(ROW_TILE, l)
    row = lambda w: pl.BlockSpec((None, tile, w), lambda bi, i: (bi, i, 0))
    whole = lambda shape: pl.BlockSpec(shape, lambda bi, i: (0,) * len(shape))
    return pl.pallas_call(
        functools.partial(_mix_kernel, tile, start, full_f32),
        grid=(b, l // tile),
        in_specs=[row(HEADS_WIDTH), row(HEADS_WIDTH), row(POOL_WIDTH), row(D_MODEL),
                  pl.BlockSpec((None, POOL_HIST, POOL_WIDTH), lambda bi, i: (bi, 0, 0)),
                  whole((POOL_WIDTH, POOL_WIDTH)), whole((1, POOL_WIDTH)), whole((D_MODEL, D_MODEL)),
                  whole((1, D_MODEL)), whole((1, D_MODEL))],
        out_specs=row(D_MODEL),
        out_shape=jax.ShapeDtypeStruct((b, l, D_MODEL), F32),
        scratch_shapes=[pltpu.VMEM((16 + tile, POOL_WIDTH), F32)],
        compiler_params=_params("arbitrary", "arbitrary"),
        name="pool_outproj_ln1",
    )(oa, ob, hc, x, pool0, pool_wbd, pool_scale, w_out, ln_g, ln_b)


def _top2_of4(a, b, c, d):
    hi1, lo1 = jnp.maximum(a, b), jnp.minimum(a, b)
    hi2, lo2 = jnp.maximum(c, d), jnp.minimum(c, d)
    return jnp.maximum(hi1, hi2), jnp.maximum(jnp.minimum(hi1, hi2), jnp.maximum(lo1, lo2))


def _route(logits_t, bias_t):
    aff = jax.nn.sigmoid(logits_t[0:N_EXPERTS, :])
    sel = aff + bias_t[0:N_EXPERTS, :]
    t = sel.shape[1]
    scores = []
    for gi in range(N_EXPERT_GROUPS):
        r = [sel[4 * gi + m:4 * gi + m + 1, :] for m in range(EXPERTS_PER_GROUP)]
        top1, top2 = _top2_of4(*r)
        scores.append(top1 + top2)
    best = scores[0]
    best_g = jnp.zeros((1, t), jnp.int32)
    for gi in range(1, N_EXPERT_GROUPS):
        better = scores[gi] > best
        best = jnp.where(better, scores[gi], best)
        best_g = jnp.where(better, gi, best_g)
    e_idx = lax.broadcasted_iota(jnp.int32, (N_EXPERTS, t), 0)
    masked = jnp.where(e_idx // EXPERTS_PER_GROUP == best_g, sel, -jnp.inf)
    m1 = jnp.max(masked, axis=0, keepdims=True)
    i1 = jnp.min(jnp.where(masked == m1, e_idx, N_EXPERTS), axis=0, keepdims=True)
    rest = jnp.where(e_idx == i1, -jnp.inf, masked)
    m2 = jnp.max(rest, axis=0, keepdims=True)
    i2 = jnp.min(jnp.where(rest == m2, e_idx, N_EXPERTS), axis=0, keepdims=True)
    w1 = jnp.sum(jnp.where(e_idx == i1, aff, 0.0), axis=0, keepdims=True)
    w2 = jnp.sum(jnp.where(e_idx == i2, aff, 0.0), axis=0, keepdims=True)
    tot = w1 + w2
    gates = jnp.where(e_idx == i1, w1 / tot, 0.0) + jnp.where(e_idx == i2, w2 / tot, 0.0)
    return jnp.concatenate([gates, jnp.zeros((LANES - N_EXPERTS, t), F32)], axis=0)


def _expert(xb, w_gu, w_d, gates, e, prec):
    gu = _dot(xb, w_gu, prec)
    hid = _silu(gu[:, 0:D_EXPERT]) * gu[:, D_EXPERT:2 * D_EXPERT]
    lane = lax.broadcasted_iota(jnp.int32, (1, LANES), 1)
    ge = jnp.sum(jnp.where(lane == e, gates, 0.0), axis=-1, keepdims=True)
    return _dot(hid.astype(w_d.dtype), w_d, prec) * ge


def _ffn_kernel(x_ref, wr_ref, rb_ref, wgu_ref, wd_ref, g_ref, b_ref, y_ref):
    x = x_ref[...]
    xb = x.astype(BF16)
    logits = _dot(xb, wr_ref[...])
    gates = _route(logits.T, rb_ref[...]).T
    acc = jnp.zeros(x.shape, F32)
    for e in range(N_EXPERTS):
        acc = acc + _expert(xb, wgu_ref[e], wd_ref[e], gates, e, None)
    y_ref[...] = _layer_norm(DEEPNORM_ALPHA * x + acc, g_ref[...], b_ref[...])


def _ffn_f32_kernel(x_ref, wr_ref, rb_ref, wgu_ref, wd_ref, g_ref, b_ref, y_ref, gates_ref, acc_ref):
    e = pl.program_id(1)
    x = x_ref[...]

    @pl.when(e == 0)
    def _():
        logits = _dot(x, wr_ref[...], HIGHEST)
        gates_ref[...] = _route(logits.T, rb_ref[...]).T
        acc_ref[...] = jnp.zeros(acc_ref.shape, F32)

    acc_ref[...] += _expert(x, wgu_ref[...], wd_ref[...], gates_ref[...], e, HIGHEST)

    @pl.when(e == N_EXPERTS - 1)
    def _():
        y_ref[...] = _layer_norm(DEEPNORM_ALPHA * x + acc_ref[...], g_ref[...], b_ref[...])


def _ffn(x2d, w_router, router_bias, w_gu, w_down, ln_g, ln_b, full_f32):
    t = x2d.shape[0]
    tm = min(ROW_TILE, t)
    out_shape = jax.ShapeDtypeStruct((t, D_MODEL), F32)
    if full_f32:
        whole = lambda shape: pl.BlockSpec(shape, lambda i, e: (0,) * len(shape))
        return pl.pallas_call(
            _ffn_f32_kernel,
            grid=(t // tm, N_EXPERTS),
            in_specs=[pl.BlockSpec((tm, D_MODEL), lambda i, e: (i, 0)),
                      whole((D_MODEL, LANES)), whole((LANES, 1)),
                      pl.BlockSpec((None, D_MODEL, 2 * D_EXPERT), lambda i, e: (e, 0, 0)),
                      pl.BlockSpec((None, D_EXPERT, D_MODEL), lambda i, e: (e, 0, 0)),
                      whole((1, D_MODEL)), whole((1, D_MODEL))],
            out_specs=pl.BlockSpec((tm, D_MODEL), lambda i, e: (i, 0)),
            out_shape=out_shape,
            scratch_shapes=[pltpu.VMEM((tm, LANES), F32), pltpu.VMEM((tm, D_MODEL), F32)],
            compiler_params=_params("arbitrary", "arbitrary"),
            name="routed_ffn_ln2_f32",
        )(x2d, w_router, router_bias, w_gu, w_down, ln_g, ln_b)
    whole = lambda shape: pl.BlockSpec(shape, lambda i: (0,) * len(shape), pipeline_mode=pl.Buffered(1))
    return pl.pallas_call(
        _ffn_kernel,
        grid=(t // tm,),
        in_specs=[pl.BlockSpec((tm, D_MODEL), lambda i: (i, 0)),
                  whole((D_MODEL, LANES)), whole((LANES, 1)),
                  whole((N_EXPERTS, D_MODEL, 2 * D_EXPERT)), whole((N_EXPERTS, D_EXPERT, D_MODEL)),
                  whole((1, D_MODEL)), whole((1, D_MODEL))],
        out_specs=pl.BlockSpec((tm, D_MODEL), lambda i: (i, 0)),
        out_shape=out_shape,
        compiler_params=_params("arbitrary"),
        name="routed_ffn_ln2",
    )(x2d, w_router, router_bias, w_gu, w_down, ln_g, ln_b)


def _prep_layer(l, wdt, w_in, w_conv, a_log, dt_bias, dn_norm_g, rel_table, pool_w, pool_scale, w_out,
                ln1_g, ln1_b, w_gate, w_up, w_down, ln2_g, ln2_b):
    wi = w_in[l]
    gates = jnp.zeros((D_MODEL, GATE_WIDTH), F32).at[:, 0:2 * N_HEADS].set(wi[:, OFF_AA:OFF_BQ])
    w_perm = jnp.concatenate([wi[:, OFF_AQ:OFF_AZ], wi[:, OFF_AZ:OFF_AA], wi[:, OFF_BQ:OFF_CU],
                              wi[:, OFF_CU:IN_WIDTH], gates], axis=1).astype(wdt)
    pad_heads = lambda vec: jnp.zeros((1, GATE_WIDTH), F32).at[0, 0:N_HEADS].set(vec)
    pool_wbd = jnp.zeros((POOL_WIDTH, POOL_WIDTH), F32)
    for gi in range(len(POOL_WINDOWS)):
        s = slice(gi * POOL_GROUP_DIM, (gi + 1) * POOL_GROUP_DIM)
        pool_wbd = pool_wbd.at[s, s].set(pool_w[l, gi])
    return dict(
        w_perm=w_perm, w_conv=w_conv[l], a_log=pad_heads(a_log[l]), dt_bias=pad_heads(dt_bias[l]),
        norm_g=jnp.tile(dn_norm_g[l], N_HEADS)[None, :], rel_table=rel_table[l],
        pool_wbd=pool_wbd, pool_scale=pool_scale[l][None, :], w_out=w_out[l].astype(wdt),
        ln1_g=ln1_g[l][None, :], ln1_b=ln1_b[l][None, :],
        w_gu=jnp.concatenate([w_gate[l], w_up[l]], axis=-1).astype(wdt), w_down=w_down[l].astype(wdt),
        ln2_g=ln2_g[l][None, :], ln2_b=ln2_b[l][None, :])


def _rel_bias(rel_table, c):
    dist = jnp.arange(c)[:, None] - (jnp.arange(BAND_ROWS + c)[None, :] - BAND_ROWS)
    return rel_table[:, jnp.clip(dist, -REL_CLIP, REL_CLIP) + REL_CLIP].astype(F32)


def _last_rows(t, n):
    if t.shape[1] < n:
        t = jnp.concatenate([jnp.zeros((t.shape[0], n - t.shape[1]) + t.shape[2:], t.dtype), t], axis=1)
    return t[:, t.shape[1] - n:]


def _trunk(x, start, dn_state, conv_state, k_hist, v_hist, pool_state, band_rows, layers, w_router, router_bias,
           full_f32):
    b, l, _ = x.shape
    assert l >= POOL_HIST
    c = min(CHUNK, l)
    new_dn, new_conv, new_k, new_v, new_pool = [], [], [], [], []
    for li, p in enumerate(layers):
        ha, hz, hb, hc, hg = _inproj(x.reshape(b * l, D_MODEL), p["w_perm"], full_f32)
        ha = ha.reshape(b, l, QKV_WIDTH)
        hz = hz.reshape(b, l, HEADS_WIDTH)
        hb = hb.reshape(b, l, QKV_WIDTH)
        hc = hc.reshape(b, l, POOL_WIDTH)
        hg = hg.reshape(b, l, GATE_WIDTH)
        oa, dn_new = _delta(ha, hz, hg, conv_state[li], dn_state[li], p["w_conv"], p["a_log"], p["dt_bias"],
                            p["norm_g"], full_f32)
        qb, kb, vb = (hb[..., j * HEADS_WIDTH:(j + 1) * HEADS_WIDTH] for j in range(3))
        bias = _rel_bias(p["rel_table"], c)
        if k_hist is None:
            ob = _attention(qb, kb, vb, None, None, bias, full_f32)
            k_new, v_new = _last_rows(kb, band_rows), _last_rows(vb, band_rows)
        else:
            kh = k_hist[li].reshape(b, -1, HEADS_WIDTH)
            vh = v_hist[li].reshape(b, -1, HEADS_WIDTH)
            ob = _attention(qb, kb, vb, kh, vh, bias, full_f32)
            k_new = _last_rows(jnp.concatenate([kh, kb], axis=1), band_rows)
            v_new = _last_rows(jnp.concatenate([vh, vb], axis=1), band_rows)
        x1 = _mix(oa, ob, hc, x, pool_state[li], p["pool_wbd"], p["pool_scale"], p["w_out"], p["ln1_g"],
                  p["ln1_b"], start, full_f32)
        x = _ffn(x1.reshape(b * l, D_MODEL), w_router.astype(p["w_out"].dtype), router_bias, p["w_gu"],
                 p["w_down"], p["ln2_g"], p["ln2_b"], full_f32).reshape(b, l, D_MODEL)
        new_dn.append(dn_new)
        new_conv.append(ha[:, l - (DN_CONV - 1):])
        new_k.append(k_new.reshape(b, band_rows, N_HEADS, HEAD_DIM))
        new_v.append(v_new.reshape(b, band_rows, N_HEADS, HEAD_DIM))
        new_pool.append(hc[:, l - POOL_HIST:])
    return (x, jnp.stack(new_dn), jnp.stack(new_conv), jnp.stack(new_k), jnp.stack(new_v), jnp.stack(new_pool))


@jax.jit
def kernel(x_prompt, x_sample, state_dn, state_conv, cache_k, cache_v, state_pool, w_in, w_conv, a_log, dt_bias,
           dn_norm_g, rel_table, pool_w, pool_scale, w_out, ln1_g, ln1_b, w_router, router_bias, w_gate, w_up,
           w_down, ln2_g, ln2_b):
    weights = (w_in, w_conv, a_log, dt_bias, dn_norm_g, rel_table, pool_w, pool_scale, w_out,
               ln1_g, ln1_b, w_gate, w_up, w_down, ln2_g, ln2_b)
    layers_bf16 = [_prep_layer(l, BF16, *weights) for l in range(DEPTH)]
    layers_f32 = [_prep_layer(l, F32, *weights) for l in range(DEPTH)]
    w_router_p = jnp.zeros((D_MODEL, LANES), F32).at[:, 0:N_EXPERTS].set(w_router)
    router_bias_p = jnp.zeros((LANES, 1), F32).at[0:N_EXPERTS, 0].set(router_bias)
    band_rows = cache_k.shape[2]
    bp = x_prompt.shape[0]
    zero_dn = jnp.zeros((DEPTH, bp, N_HEADS, HEAD_DIM, HEAD_DIM), state_dn.dtype)
    zero_conv = jnp.zeros((DEPTH, bp, DN_CONV - 1, QKV_WIDTH), x_prompt.dtype)
    zero_pool = jnp.zeros((DEPTH, bp, POOL_HIST, POOL_WIDTH), x_prompt.dtype)
    prompt = _trunk(x_prompt, 0, zero_dn, zero_conv, None, None, zero_pool, band_rows, layers_bf16, w_router_p,
                    router_bias_p, False)
    sample = _trunk(x_sample, PAST_LEN, state_dn, state_conv, cache_k, cache_v, state_pool, band_rows, layers_f32,
                    w_router_p, router_bias_p, True)
    return (prompt[0], sample[0]) + prompt[1:] + sample[1:]
```

```python
import functools
import math

import jax
import jax.numpy as jnp
from jax import lax
from jax.experimental import pallas as pl
from jax.experimental.pallas import tpu as pltpu

F32 = jnp.float32
BF16 = jnp.bfloat16
HIGHEST = lax.Precision.HIGHEST

D_MODEL = 1024
HEAD_DIM = 64
N_HEADS = 6
HEADS_WIDTH = N_HEADS * HEAD_DIM
QKV_WIDTH = 3 * HEADS_WIDTH
DN_CONV = 4
CHUNK = 64
BAND_ROWS = 512
REL_CLIP = 128
POOL_WINDOWS = (2, 4, 8, 16)
POOL_GROUP_DIM = 64
POOL_WIDTH = 256
POOL_HIST = 15
N_EXPERTS = 16
N_EXPERT_GROUPS = 4
EXPERTS_PER_GROUP = 4
D_EXPERT = 256
DEPTH = 2
DEEPNORM_ALPHA = (2 * DEPTH) ** 0.25
LN_EPS = 1e-5
RMS_EPS = 1e-6
PAST_LEN = 1024

OFF_AQ = 0
OFF_AZ = 3 * HEADS_WIDTH
OFF_AA = OFF_AZ + HEADS_WIDTH
OFF_AB = OFF_AA + N_HEADS
OFF_BQ = OFF_AB + N_HEADS
OFF_CU = OFF_BQ + 3 * HEADS_WIDTH
IN_WIDTH = OFF_CU + POOL_WIDTH

LANES = 128
GATE_WIDTH = LANES
SEG_A = (0, QKV_WIDTH)
SEG_Z = (SEG_A[1], SEG_A[1] + HEADS_WIDTH)
SEG_B = (SEG_Z[1], SEG_Z[1] + QKV_WIDTH)
SEG_C = (SEG_B[1], SEG_B[1] + POOL_WIDTH)
SEG_G = (SEG_C[1], SEG_C[1] + GATE_WIDTH)
PERM_WIDTH = SEG_G[1]

ROW_TILE = 512
VMEM_LIMIT = 56 * 1024 * 1024


def _params(*sem):
    return pltpu.CompilerParams(dimension_semantics=sem, vmem_limit_bytes=VMEM_LIMIT)


def _dot(a, b, precision=None):
    return jnp.dot(a, b, preferred_element_type=F32, precision=precision)


def _dot_nt(a, b, precision=None):
    return lax.dot_general(a, b, (((1,), (1,)), ((), ())), preferred_element_type=F32, precision=precision)


def _dot_tn(a, b, precision=None):
    return lax.dot_general(a, b, (((0,), (0,)), ((), ())), preferred_element_type=F32, precision=precision)


def _silu(x):
    return x * jax.nn.sigmoid(x)


def _layer_norm(x, g, b):
    mu = jnp.mean(x, axis=-1, keepdims=True)
    xc = x - mu
    var = jnp.mean(xc * xc, axis=-1, keepdims=True)
    return xc * lax.rsqrt(var + LN_EPS) * g + b


def _inproj_kernel(full_f32, x_ref, w_ref, ha_ref, hz_ref, hb_ref, hc_ref, hg_ref):
    prec = HIGHEST if full_f32 else None
    xb = x_ref[...].astype(w_ref.dtype)
    for (lo, hi), out_ref in ((SEG_A, ha_ref), (SEG_Z, hz_ref), (SEG_B, hb_ref), (SEG_C, hc_ref), (SEG_G, hg_ref)):
        step = 384 if (hi - lo) % 384 == 0 else hi - lo
        for c0 in range(0, hi - lo, step):
            out_ref[:, c0:c0 + step] = _dot(xb, w_ref[:, lo + c0:lo + c0 + step], prec)


def _inproj(x2d, w_perm, full_f32):
    t = x2d.shape[0]
    tm = min(ROW_TILE, t)
    widths = [s[1] - s[0] for s in (SEG_A, SEG_Z, SEG_B, SEG_C, SEG_G)]
    return pl.pallas_call(
        functools.partial(_inproj_kernel, full_f32),
        grid=(t // tm,),
        in_specs=[pl.BlockSpec((tm, D_MODEL), lambda i: (i, 0)),
                  pl.BlockSpec((D_MODEL, PERM_WIDTH), lambda i: (0, 0))],
        out_specs=[pl.BlockSpec((tm, w), lambda i: (i, 0)) for w in widths],
        out_shape=[jax.ShapeDtypeStruct((t, w), F32) for w in widths],
        compiler_params=_params("arbitrary"),
        name="inproj",
    )(x2d, w_perm)


def _inproj_seq_kernel(x_ref, w_ref, ha_ref, hz_ref, hb_ref, hc_ref, hg_ref, kv_ref):
    xb = x_ref[...].astype(BF16)
    step = HEADS_WIDTH
    for (lo, hi), out_ref in ((SEG_A, ha_ref), (SEG_Z, hz_ref), (SEG_C, hc_ref), (SEG_G, hg_ref)):
        for c0 in range(0, hi - lo, step):
            c1 = min(c0 + step, hi - lo)
            out_ref[:, c0:c1] = _dot(xb, w_ref[:, lo + c0:lo + c1])
    for j in range(3):
        h = _dot(xb, w_ref[:, SEG_B[0] + j * step:SEG_B[0] + (j + 1) * step])
        hb_ref[:, j * step:(j + 1) * step] = h.astype(BF16)
        if j > 0:
            kv_ref[:, (j - 1) * step:j * step] = h


def _inproj_seq(x, w_perm):
    b, l, _ = x.shape
    tm = ROW_TILE
    assert tm == BAND_ROWS and l % tm == 0
    widths = [s[1] - s[0] for s in (SEG_A, SEG_Z, SEG_B, SEG_C, SEG_G)]
    dtypes = [F32, F32, BF16, F32, F32]
    row = lambda w: pl.BlockSpec((None, tm, w), lambda bi, i: (bi, i, 0))
    return pl.pallas_call(
        _inproj_seq_kernel,
        grid=(b, l // tm),
        in_specs=[row(D_MODEL), pl.BlockSpec((D_MODEL, PERM_WIDTH), lambda bi, i: (0, 0))],
        out_specs=[row(w) for w in widths] + [pl.BlockSpec((None, tm, 2 * HEADS_WIDTH), lambda bi, i: (bi, 0, 0))],
        out_shape=[jax.ShapeDtypeStruct((b, l, w), d) for w, d in zip(widths, dtypes)]
        + [jax.ShapeDtypeStruct((b, tm, 2 * HEADS_WIDTH), F32)],
        compiler_params=_params("arbitrary", "arbitrary"),
        name="inproj_seq",
    )(x, w_perm)


def _delta_kernel(c, full_f32, ha_ref, hz_ref, hg_ref, conv0_ref, dn0_ref, wconv_ref, alog_ref, dtb_ref, ng_ref,
                  oa_ref, dn_ref, s_ref, ext_ref):
    n = pl.program_id(1)
    prec = HIGHEST if full_f32 else None
    pad = 8 - (DN_CONV - 1)

    @pl.when(n == 0)
    def _():
        s_ref[...] = dn0_ref[...]
        ext_ref[pad:8, :] = conv0_ref[...]

    u = ha_ref[...]
    ext_ref[8:8 + c, :] = u
    conv = ext_ref[pad:pad + c, :] * wconv_ref[0:1, :]
    for j in range(1, DN_CONV):
        conv = conv + ext_ref[pad + j:pad + j + c, :] * wconv_ref[j:j + 1, :]
    ext_ref[pad:8, :] = u[c - (DN_CONV - 1):c, :]
    qkv = _silu(conv)
    q = qkv[:, 0:HEADS_WIDTH]
    k = qkv[:, HEADS_WIDTH:2 * HEADS_WIDTH]
    v = qkv[:, 2 * HEADS_WIDTH:QKV_WIDTH]

    r = lax.broadcasted_iota(jnp.int32, (HEADS_WIDTH, HEADS_WIDTH), 0) // HEAD_DIM
    cc = lax.broadcasted_iota(jnp.int32, (HEADS_WIDTH, HEADS_WIDTH), 1) // HEAD_DIM
    head_ones = (r == cc).astype(F32)
    qn = q * lax.rsqrt(_dot(q * q, head_ones, HIGHEST) + RMS_EPS) * HEAD_DIM ** -0.5
    kn = k * lax.rsqrt(_dot(k * k, head_ones, HIGHEST) + RMS_EPS)

    hg = hg_ref[...]
    lane = lax.broadcasted_iota(jnp.int32, (1, GATE_WIDTH), 1)
    neg_rate = jnp.where(lane < N_HEADS, -jnp.exp(alog_ref[...]), 0.0)
    g = neg_rate * jax.nn.softplus(hg + dtb_ref[...])
    beta = jax.nn.sigmoid(hg)
    ri = lax.broadcasted_iota(jnp.int32, (c, c), 0)
    ci = lax.broadcasted_iota(jnp.int32, (c, c), 1)
    incl = ri >= ci
    strict = ri > ci
    gc = _dot(incl.astype(F32), g, HIGHEST)
    gc_t = _dot_tn(g, (ri <= ci).astype(F32), HIGHEST)

    z = hz_ref[...]
    for h in range(N_HEADS):
        sl = slice(h * HEAD_DIM, (h + 1) * HEAD_DIM)
        qh, kh, vh = qn[:, sl], kn[:, sl], v[:, sl]
        gch = gc[:, h:h + 1]
        diff = gch - gc_t[h:h + 1, :]
        decay = jnp.where(incl, jnp.exp(jnp.where(incl, diff, 0.0)), 0.0)
        bh = beta[:, N_HEADS + h:N_HEADS + h + 1]
        kbeta = kh * bh
        eg = jnp.exp(gch)
        neg_m = jnp.where(strict, -(_dot_nt(kbeta, kh, prec) * decay), 0.0)
        sol = jnp.concatenate([vh * bh, kbeta * eg], axis=-1)
        power = neg_m
        sol = sol + _dot(power, sol, prec)
        for _ in range(int(math.log2(c)) - 1):
            power = _dot(power, power, prec)
            sol = sol + _dot(power, sol, prec)
        u_h, w_h = sol[:, 0:HEAD_DIM], sol[:, HEAD_DIM:2 * HEAD_DIM]
        a_intra = jnp.where(incl, _dot_nt(qh, kh, prec) * decay, 0.0)
        g_last = gc[c - 1:c, h:h + 1]
        k_tail = kh * jnp.exp(g_last - gch)
        s = s_ref[h]
        v_new = u_h - _dot(w_h, s, prec)
        o = _dot(qh * eg, s, prec) + _dot(a_intra, v_new, prec)
        s_ref[h] = s * jnp.exp(g_last) + _dot_tn(k_tail, v_new, prec)
        o = o * lax.rsqrt(jnp.mean(o * o, axis=-1, keepdims=True) + RMS_EPS)
        oa_ref[:, sl] = o * ng_ref[:, sl] * _silu(z[:, sl])

    @pl.when(n == pl.num_programs(1) - 1)
    def _():
        dn_ref[...] = s_ref[...]


def _delta(ha, hz, hg, conv0, dn0, w_conv, a_log, dt_bias, norm_g, full_f32):
    b, l, _ = ha.shape
    c = min(CHUNK, l)
    row = lambda w: pl.BlockSpec((None, c, w), lambda i, n: (i, n, 0))
    whole = lambda shape: pl.BlockSpec(shape, lambda i, n: (0,) * len(shape))
    state = pl.BlockSpec((None, N_HEADS, HEAD_DIM, HEAD_DIM), lambda i, n: (i, 0, 0, 0))
    return pl.pallas_call(
        functools.partial(_delta_kernel, c, full_f32),
        grid=(b, l // c),
        in_specs=[row(QKV_WIDTH), row(HEADS_WIDTH), row(GATE_WIDTH),
                  pl.BlockSpec((None, DN_CONV - 1, QKV_WIDTH), lambda i, n: (i, 0, 0)), state,
                  whole((DN_CONV, QKV_WIDTH)), whole((1, GATE_WIDTH)), whole((1, GATE_WIDTH)),
                  whole((1, HEADS_WIDTH))],
        out_specs=[row(HEADS_WIDTH), state],
        out_shape=[jax.ShapeDtypeStruct((b, l, HEADS_WIDTH), F32),
                   jax.ShapeDtypeStruct((b, N_HEADS, HEAD_DIM, HEAD_DIM), F32)],
        scratch_shapes=[pltpu.VMEM((N_HEADS, HEAD_DIM, HEAD_DIM), F32),
                        pltpu.VMEM((8 + c, QKV_WIDTH), F32)],
        compiler_params=_params("arbitrary", "arbitrary"),
        name="delta_rule",
    )(ha, hz, hg, conv0, dn0, w_conv, a_log, dt_bias, norm_g)


PAIR_ROWS = 2 * CHUNK
N_PAIRS = N_HEADS // 2


def _delta_pairs_kernel(ha_ref, hz_ref, hg_ref, conv0_ref, dn0_ref, wconv_ref, alog_ref, dtb_ref, ng_ref,
                        oa_ref, dn_ref, z_ref, ext_ref):
    n = pl.program_id(1)
    c, rows = CHUNK, PAIR_ROWS
    pad = 8 - (DN_CONV - 1)
    lane = lax.broadcasted_iota(jnp.int32, (1, LANES), 1)
    lo = lane < HEAD_DIM
    zero64 = jnp.zeros((HEAD_DIM, HEAD_DIM), F32)

    @pl.when(n == 0)
    def _():
        for p in range(N_PAIRS):
            top = jnp.concatenate([zero64, dn0_ref[2 * p + 1]], axis=1)
            bot = jnp.concatenate([dn0_ref[2 * p], zero64], axis=1)
            z_ref[p] = jnp.concatenate([top, bot], axis=0)
        ext_ref[pad:8, :] = conv0_ref[...]

    u = ha_ref[...]
    ext_ref[8:8 + rows, :] = u
    conv = ext_ref[pad:pad + rows, :] * wconv_ref[0:1, :]
    for j in range(1, DN_CONV):
        conv = conv + ext_ref[pad + j:pad + j + rows, :] * wconv_ref[j:j + 1, :]
    ext_ref[pad:8, :] = u[rows - (DN_CONV - 1):rows, :]
    qkv = _silu(conv)
    q = qkv[:, 0:HEADS_WIDTH]
    k = qkv[:, HEADS_WIDTH:2 * HEADS_WIDTH]
    v = qkv[:, 2 * HEADS_WIDTH:QKV_WIDTH]
    hr = lax.broadcasted_iota(jnp.int32, (HEADS_WIDTH, HEADS_WIDTH), 0) // HEAD_DIM
    hc = lax.broadcasted_iota(jnp.int32, (HEADS_WIDTH, HEADS_WIDTH), 1) // HEAD_DIM
    head_ones = (hr == hc).astype(BF16)
    qn = q * lax.rsqrt(_dot((q * q).astype(BF16), head_ones) + RMS_EPS) * HEAD_DIM ** -0.5
    kn = k * lax.rsqrt(_dot((k * k).astype(BF16), head_ones) + RMS_EPS)
    pr = lax.broadcasted_iota(jnp.int32, (LANES, LANES), 0) // HEAD_DIM
    pc = lax.broadcasted_iota(jnp.int32, (LANES, LANES), 1) // HEAD_DIM
    pair_ones = (pr == pc).astype(BF16)
    anti = pr != pc

    hg = hg_ref[...]
    neg_rate = jnp.where(lane < N_HEADS, -jnp.exp(alog_ref[...]), 0.0)
    g = neg_rate * jax.nn.softplus(hg + dtb_ref[...])
    beta = jax.nn.sigmoid(hg)
    ri = lax.broadcasted_iota(jnp.int32, (rows, rows), 0)
    ci = lax.broadcasted_iota(jnp.int32, (rows, rows), 1)
    gc = _dot(((ri >= ci) & (ri // c == ci // c)).astype(F32), g, HIGHEST)
    gc_t = gc.T[0:8, :]
    gc_t_sw = pltpu.roll(gc_t, HEAD_DIM, axis=1)

    row_i = lax.broadcasted_iota(jnp.int32, (c, 1), 0)
    col_j = lane % HEAD_DIM
    incl = row_i >= col_j
    strict = row_i > col_j
    z = hz_ref[...]
    swap = lambda x: jnp.concatenate([x[HEAD_DIM:], x[:HEAD_DIM]], axis=0)
    zeros_cl = jnp.zeros((c, LANES), F32)

    for p in range(N_PAIRS):
        e, o = 2 * p, 2 * p + 1
        ls = slice(p * LANES, (p + 1) * LANES)
        zst = z_ref[p]
        for s in range(rows // c):
            rs = slice(s * c, (s + 1) * c)
            kp, vp, qp = kn[rs, ls], v[rs, ls], qn[rs, ls]
            ksw = pltpu.roll(kp, HEAD_DIM, axis=1)
            qsw = pltpu.roll(qp, HEAD_DIM, axis=1)
            g_e, g_o = gc[rs, e:e + 1], gc[rs, o:o + 1]
            b_e, b_o = beta[rs, N_HEADS + e:N_HEADS + e + 1], beta[rs, N_HEADS + o:N_HEADS + o + 1]
            gl_e, gl_o = g_e[c - 1:c, :], g_o[c - 1:c, :]
            eg_e, eg_o = jnp.exp(g_e), jnp.exp(g_o)
            if s == 0:
                g_row = jnp.where(lo, gc_t[e:e + 1, :], gc_t_sw[o:o + 1, :])
            else:
                g_row = jnp.where(lo, gc_t_sw[e:e + 1, :], gc_t[o:o + 1, :])
            diff = jnp.where(lo, g_e, g_o) - g_row
            decay = jnp.where(incl, jnp.exp(jnp.where(incl, diff, 0.0)), 0.0)
            kb = kp * jnp.where(lo, b_e, b_o)
            k_blocks = jnp.concatenate([jnp.where(lo, kp, 0.0), jnp.where(lo, 0.0, kp)], axis=0).astype(BF16)
            st = _dot_nt(jnp.concatenate([kb, qp], axis=0).astype(BF16), k_blocks)
            power = jnp.where(strict, -(st[0:c] * decay), 0.0)
            a2 = jnp.where(incl, st[c:2 * c] * decay, 0.0)
            sol_e = jnp.where(lo, vp, ksw * eg_e) * b_e
            sol_o = jnp.where(lo, ksw * eg_o, vp) * b_o
            n_stage = int(math.log2(c))
            for stage in range(n_stage):
                tail = stage == n_stage - 1
                top = [sol_e, zeros_cl] + ([] if tail else [jnp.where(lo, power, 0.0)])
                bot = [zeros_cl, sol_o] + ([] if tail else [jnp.where(lo, 0.0, power)])
                rhs = jnp.concatenate([jnp.concatenate(top, axis=1), jnp.concatenate(bot, axis=1)], axis=0)
                res = _dot(power.astype(BF16), rhs.astype(BF16))
                sol_e = sol_e + res[:, 0:LANES]
                sol_o = sol_o + res[:, LANES:2 * LANES]
                if not tail:
                    power = res[:, 2 * LANES:3 * LANES]
            top = jnp.concatenate([jnp.where(lo, sol_e, 0.0), jnp.where(lo, 0.0, sol_e)], axis=1)
            bot = jnp.concatenate([jnp.where(lo, 0.0, sol_o), jnp.where(lo, sol_o, 0.0)], axis=1)
            y = _dot(a2.astype(BF16), jnp.concatenate([top, bot], axis=0).astype(BF16))
            qe_sw = qsw * jnp.where(lo, eg_o, eg_e) - y[:, LANES:2 * LANES]
            kt = kp * jnp.exp(jnp.where(lo, gl_e - g_e, gl_o - g_o))
            kt_blocks = jnp.concatenate([jnp.where(lo, kt, 0.0), jnp.where(lo, 0.0, kt)], axis=0).astype(BF16)
            kbm = _dot_tn(kt_blocks, jnp.concatenate([sol_e, sol_o], axis=0).astype(BF16))
            zb = zst.astype(BF16)
            out = y[:, 0:LANES] + _dot(qe_sw.astype(BF16), zb)
            ks = _dot(jnp.where(anti, kbm, 0.0).astype(BF16), zb)
            zst = zst * jnp.exp(jnp.where(lo, gl_e, gl_o)) - swap(ks) + swap(jnp.where(anti, 0.0, kbm))
            ms = _dot((out * out).astype(BF16), pair_ones) * (1.0 / HEAD_DIM)
            oa_ref[rs, ls] = out * lax.rsqrt(ms + RMS_EPS) * ng_ref[:, ls] * _silu(z[rs, ls])
        z_ref[p] = zst

    @pl.when(n == pl.num_programs(1) - 1)
    def _():
        for p in range(N_PAIRS):
            dn_ref[2 * p] = z_ref[p][HEAD_DIM:, 0:HEAD_DIM]
            dn_ref[2 * p + 1] = z_ref[p][0:HEAD_DIM, HEAD_DIM:]


def _delta_pairs(ha, hz, hg, conv0, dn0, w_conv, a_log, dt_bias, norm_g):
    b, l, _ = ha.shape
    rows = PAIR_ROWS
    row = lambda w: pl.BlockSpec((None, rows, w), lambda i, n: (i, n, 0))
    whole = lambda shape: pl.BlockSpec(shape, lambda i, n: (0,) * len(shape))
    state = pl.BlockSpec((None, N_HEADS, HEAD_DIM, HEAD_DIM), lambda i, n: (i, 0, 0, 0))
    return pl.pallas_call(
        _delta_pairs_kernel,
        grid=(b, l // rows),
        in_specs=[row(QKV_WIDTH), row(HEADS_WIDTH), row(GATE_WIDTH),
                  pl.BlockSpec((None, DN_CONV - 1, QKV_WIDTH), lambda i, n: (i, 0, 0)), state,
                  whole((DN_CONV, QKV_WIDTH)), whole((1, GATE_WIDTH)), whole((1, GATE_WIDTH)),
                  whole((1, HEADS_WIDTH))],
        out_specs=[row(HEADS_WIDTH), state],
        out_shape=[jax.ShapeDtypeStruct((b, l, HEADS_WIDTH), F32),
                   jax.ShapeDtypeStruct((b, N_HEADS, HEAD_DIM, HEAD_DIM), F32)],
        scratch_shapes=[pltpu.VMEM((N_PAIRS, LANES, LANES), F32),
                        pltpu.VMEM((8 + rows, QKV_WIDTH), F32)],
        compiler_params=_params("arbitrary", "arbitrary"),
        name="delta_rule_pairs",
    )(ha, hz, hg, conv0, dn0, w_conv, a_log, dt_bias, norm_g)


def _attn_kernel(c, tq, mask_before_start, full_f32, q_ref, kp_ref, ks_ref, vp_ref, vs_ref, bias_ref, o_ref,
                 kwin_ref, vwin_ref):
    i = pl.program_id(1)
    prec = HIGHEST if full_f32 else None
    w = BAND_ROWS
    kwin_ref[0:w, :] = kp_ref[...]
    kwin_ref[w:w + tq, :] = ks_ref[...]
    vwin_ref[0:w, :] = vp_ref[...]
    vwin_ref[w:w + tq, :] = vs_ref[...]
    key_idx = lax.broadcasted_iota(jnp.int32, (1, w + c), 1)

    def chunk(j, carry):
        r0 = pl.multiple_of(j * c, c)
        q = q_ref[pl.ds(r0, c), :]
        kk = kwin_ref[pl.ds(r0, w + c), :]
        vv = vwin_ref[pl.ds(r0, w + c), :]
        valid = (i * tq + r0 + key_idx - w) >= 0
        for h in range(N_HEADS):
            sl = slice(h * HEAD_DIM, (h + 1) * HEAD_DIM)
            s = _dot_nt(q[:, sl], kk[:, sl], prec) * HEAD_DIM ** -0.5 + bias_ref[h]
            if mask_before_start:
                s = jnp.where(valid, s, -1e30)
            s = s - jnp.max(s, axis=-1, keepdims=True)
            p = jnp.exp(s)
            p = p / jnp.sum(p, axis=-1, keepdims=True)
            o_ref[pl.ds(r0, c), sl] = _dot(p, vv[:, sl], prec)
        return carry

    lax.fori_loop(0, tq // c, chunk, 0)


def _attention(q, k, v, k_hist, v_hist, bias, full_f32):
    b, l, _ = q.shape
    c = min(CHUNK, l)
    tq = min(ROW_TILE, l)
    assert tq == BAND_ROWS or l == tq
    own = pl.BlockSpec((None, tq, HEADS_WIDTH), lambda bi, i: (bi, i, 0))
    if k_hist is None:
        prev = pl.BlockSpec((None, BAND_ROWS, HEADS_WIDTH), lambda bi, i: (bi, jnp.maximum(i - 1, 0), 0))
        k_prev, v_prev = k, v
    else:
        prev = pl.BlockSpec((None, BAND_ROWS, HEADS_WIDTH), lambda bi, i: (bi, 0, 0))
        k_prev, v_prev = k_hist, v_hist
    return pl.pallas_call(
        functools.partial(_attn_kernel, c, tq, k_hist is None, full_f32),
        grid=(b, l // tq),
        in_specs=[own, prev, own, prev, own,
                  pl.BlockSpec((N_HEADS, c, BAND_ROWS + c), lambda bi, i: (0, 0, 0))],
        out_specs=own,
        out_shape=jax.ShapeDtypeStruct((b, l, HEADS_WIDTH), F32),
        scratch_shapes=[pltpu.VMEM((BAND_ROWS + tq, HEADS_WIDTH), F32),
                        pltpu.VMEM((BAND_ROWS + tq, HEADS_WIDTH), F32)],
        compiler_params=_params("arbitrary", "arbitrary"),
        name="band_attention",
    )(q, k_prev, k, v_prev, v, bias)


STAGED_ROWS = 256
STAGED_CHUNKS = STAGED_ROWS // CHUNK
STAGED_PROBLEMS = STAGED_CHUNKS * N_PAIRS


def _delta_staged_kernel(ha_ref, hz_ref, hg_ref, conv0_ref, dn0_ref, wconv_ref, alog_ref, dtb_ref, ng_ref,
                         tril_ref, hones_ref, pones_ref,
                         oa_ref, dn_ref,
                         z_ref, ext_ref, qn_ref, kn_ref, v_ref, gc_ref, beta_ref, gct_ref,
                         pw_ref, sol_ref, a2_ref, ob_ref, qe_ref, kmat_ref, bsw_ref, egl_ref):
    n = pl.program_id(1)
    c, rows = CHUNK, STAGED_ROWS
    pad = 8 - (DN_CONV - 1)
    lane = lax.broadcasted_iota(jnp.int32, (1, LANES), 1)
    lo = lane < HEAD_DIM
    zero64 = jnp.zeros((HEAD_DIM, HEAD_DIM), F32)

    @pl.when(n == 0)
    def _():
        for p in range(N_PAIRS):
            top = jnp.concatenate([zero64, dn0_ref[2 * p + 1]], axis=1)
            bot = jnp.concatenate([dn0_ref[2 * p], zero64], axis=1)
            z_ref[p] = jnp.concatenate([top, bot], axis=0)
        ext_ref[pad:8, :] = conv0_ref[...]

    u = ha_ref[...]
    ext_ref[8:8 + rows, :] = u
    conv = ext_ref[pad:pad + rows, :] * wconv_ref[0:1, :]
    for j in range(1, DN_CONV):
        conv = conv + ext_ref[pad + j:pad + j + rows, :] * wconv_ref[j:j + 1, :]
    ext_ref[pad:8, :] = u[rows - (DN_CONV - 1):rows, :]
    qkv = _silu(conv)
    q = qkv[:, 0:HEADS_WIDTH]
    k = qkv[:, HEADS_WIDTH:2 * HEADS_WIDTH]
    qn_ref[...] = q * lax.rsqrt(_dot((q * q).astype(BF16), hones_ref[...]) + RMS_EPS) * HEAD_DIM ** -0.5
    kn_ref[...] = k * lax.rsqrt(_dot((k * k).astype(BF16), hones_ref[...]) + RMS_EPS)
    v_ref[...] = qkv[:, 2 * HEADS_WIDTH:QKV_WIDTH]

    hg = hg_ref[...]
    neg_rate = jnp.where(lane < N_HEADS, -jnp.exp(alog_ref[...]), 0.0)
    gc = _dot(tril_ref[...], neg_rate * jax.nn.softplus(hg + dtb_ref[...]), HIGHEST)
    gc_ref[...] = gc
    beta_ref[...] = jax.nn.sigmoid(hg)
    for blk in range(rows // LANES):
        gct_ref[blk] = gc[blk * LANES:(blk + 1) * LANES, :].T[0:8, :]

    row_i = lax.broadcasted_iota(jnp.int32, (c, 1), 0)
    col_j = lane % HEAD_DIM
    incl = row_i >= col_j
    strict = row_i > col_j
    pr = lax.broadcasted_iota(jnp.int32, (LANES, LANES), 0) // HEAD_DIM
    pc = lax.broadcasted_iota(jnp.int32, (LANES, LANES), 1) // HEAD_DIM
    anti = pr != pc
    swap = lambda x: jnp.concatenate([x[HEAD_DIM:], x[:HEAD_DIM]], axis=0)
    halves = lambda x: jnp.concatenate([jnp.where(lo, x, jnp.zeros_like(x)), jnp.where(lo, jnp.zeros_like(x), x)],
                                       axis=0)
    problems = [(s, p) for s in range(STAGED_CHUNKS) for p in range(N_PAIRS)]

    def gate_cols(s, p):
        rs = slice(s * c, (s + 1) * c)
        e, o = 2 * p, 2 * p + 1
        gcs = gc_ref[rs, :]
        g_e, g_o = gcs[:, e:e + 1], gcs[:, o:o + 1]
        return g_e, g_o, g_e[c - 1:c, :], g_o[c - 1:c, :]

    for gi, (s, p) in enumerate(problems):
        rs = slice(s * c, (s + 1) * c)
        ls = slice(p * LANES, (p + 1) * LANES)
        e, o = 2 * p, 2 * p + 1
        kp, vp, qp = kn_ref[rs, ls], v_ref[rs, ls], qn_ref[rs, ls]
        ksw = pltpu.roll(kp, HEAD_DIM, axis=1)
        g_e, g_o, _, _ = gate_cols(s, p)
        betas = beta_ref[rs, :]
        b_e, b_o = betas[:, N_HEADS + e:N_HEADS + e + 1], betas[:, N_HEADS + o:N_HEADS + o + 1]
        gt = gct_ref[s // 2]
        gt_sw = pltpu.roll(gt, HEAD_DIM, axis=1)
        if s % 2 == 0:
            g_row = jnp.where(lo, gt[e:e + 1, :], gt_sw[o:o + 1, :])
        else:
            g_row = jnp.where(lo, gt_sw[e:e + 1, :], gt[o:o + 1, :])
        diff = jnp.where(lo, g_e, g_o) - g_row
        decay = jnp.where(incl, jnp.exp(jnp.where(incl, diff, 0.0)), 0.0)
        kb = kp * jnp.where(lo, b_e, b_o)
        st = _dot_nt(jnp.concatenate([kb, qp], axis=0).astype(BF16), halves(kp.astype(BF16)))
        pw_ref[gi] = jnp.where(strict, -(st[0:c] * decay), 0.0).astype(BF16)
        a2_ref[gi] = jnp.where(incl, st[c:2 * c] * decay, 0.0).astype(BF16)
        sol_ref[gi, :, 0:LANES] = jnp.where(lo, vp, ksw * jnp.exp(g_e)) * b_e
        sol_ref[gi, :, LANES:2 * LANES] = jnp.where(lo, ksw * jnp.exp(g_o), vp) * b_o

    zeros_cl = jnp.zeros((c, LANES), BF16)
    n_stage = int(math.log2(c))
    for stage in range(n_stage):
        tail = stage == n_stage - 1
        for gi in range(STAGED_PROBLEMS):
            power = pw_ref[gi]
            sol = sol_ref[gi]
            sb = sol.astype(BF16)
            top = [sb[:, 0:LANES], zeros_cl] + ([] if tail else [jnp.where(lo, power, zeros_cl)])
            bot = [zeros_cl, sb[:, LANES:2 * LANES]] + ([] if tail else [jnp.where(lo, zeros_cl, power)])
            rhs = jnp.concatenate([jnp.concatenate(top, axis=1), jnp.concatenate(bot, axis=1)], axis=0)
            res = _dot(power, rhs)
            sol_ref[gi] = sol + res[:, 0:2 * LANES]
            if not tail:
                pw_ref[gi] = res[:, 2 * LANES:3 * LANES].astype(BF16)

    for gi, (s, p) in enumerate(problems):
        rs = slice(s * c, (s + 1) * c)
        ls = slice(p * LANES, (p + 1) * LANES)
        sol = sol_ref[gi]
        sol_e, sol_o = sol[:, 0:LANES], sol[:, LANES:2 * LANES]
        g_e, g_o, gl_e, gl_o = gate_cols(s, p)
        top = jnp.concatenate([jnp.where(lo, sol_e, 0.0), jnp.where(lo, 0.0, sol_e)], axis=1)
        bot = jnp.concatenate([jnp.where(lo, 0.0, sol_o), jnp.where(lo, sol_o, 0.0)], axis=1)
        y = _dot(a2_ref[gi], jnp.concatenate([top, bot], axis=0).astype(BF16))
        ob_ref[gi] = y[:, 0:LANES]
        qsw = pltpu.roll(qn_ref[rs, ls], HEAD_DIM, axis=1)
        qe_ref[gi] = (qsw * jnp.where(lo, jnp.exp(g_o), jnp.exp(g_e)) - y[:, LANES:2 * LANES]).astype(BF16)
        kt = kn_ref[rs, ls] * jnp.exp(jnp.where(lo, gl_e - g_e, gl_o - g_o))
        kbm = _dot_tn(halves(kt.astype(BF16)), jnp.concatenate([sol_e, sol_o], axis=0).astype(BF16))
        kmat_ref[gi] = jnp.where(anti, kbm, 0.0).astype(BF16)
        bsw_ref[gi] = swap(jnp.where(anti, 0.0, kbm))
        egl_ref[gi] = jnp.exp(jnp.where(lo, gl_e, gl_o))

    for s in range(STAGED_CHUNKS):
        rs = slice(s * c, (s + 1) * c)
        for p in range(N_PAIRS):
            gi = s * N_PAIRS + p
            ls = slice(p * LANES, (p + 1) * LANES)
            zst = z_ref[p]
            zb = zst.astype(BF16)
            out = ob_ref[gi] + _dot(qe_ref[gi], zb)
            z_ref[p] = zst * egl_ref[gi] - swap(_dot(kmat_ref[gi], zb)) + bsw_ref[gi]
            ms = _dot((out * out).astype(BF16), pones_ref[...]) * (1.0 / HEAD_DIM)
            oa_ref[rs, ls] = out * lax.rsqrt(ms + RMS_EPS) * ng_ref[:, ls] * _silu(hz_ref[rs, ls])

    @pl.when(n == pl.num_programs(1) - 1)
    def _():
        for p in range(N_PAIRS):
            dn_ref[2 * p] = z_ref[p][HEAD_DIM:, 0:HEAD_DIM]
            dn_ref[2 * p + 1] = z_ref[p][0:HEAD_DIM, HEAD_DIM:]


def _delta_staged(ha, hz, hg, conv0, dn0, w_conv, a_log, dt_bias, norm_g):
    b, l, _ = ha.shape
    rows, c, g = STAGED_ROWS, CHUNK, STAGED_PROBLEMS
    ri, ci = jnp.arange(rows)[:, None], jnp.arange(rows)[None, :]
    tril = ((ri >= ci) & (ri // c == ci // c)).astype(F32)
    hi, hj = jnp.arange(HEADS_WIDTH)[:, None] // HEAD_DIM, jnp.arange(HEADS_WIDTH)[None, :] // HEAD_DIM
    head_ones = (hi == hj).astype(BF16)
    pair_ones = head_ones[0:LANES, 0:LANES]
    row = lambda w: pl.BlockSpec((None, rows, w), lambda i, n: (i, n, 0))
    whole = lambda shape: pl.BlockSpec(shape, lambda i, n: (0,) * len(shape))
    state = pl.BlockSpec((None, N_HEADS, HEAD_DIM, HEAD_DIM), lambda i, n: (i, 0, 0, 0))
    return pl.pallas_call(
        _delta_staged_kernel,
        grid=(b, l // rows),
        in_specs=[row(QKV_WIDTH), row(HEADS_WIDTH), row(GATE_WIDTH),
                  pl.BlockSpec((None, DN_CONV - 1, QKV_WIDTH), lambda i, n: (i, 0, 0)), state,
                  whole((DN_CONV, QKV_WIDTH)), whole((1, GATE_WIDTH)), whole((1, GATE_WIDTH)),
                  whole((1, HEADS_WIDTH)), whole((rows, rows)), whole((HEADS_WIDTH, HEADS_WIDTH)),
                  whole((LANES, LANES))],
        out_specs=[row(HEADS_WIDTH), state],
        out_shape=[jax.ShapeDtypeStruct((b, l, HEADS_WIDTH), F32),
                   jax.ShapeDtypeStruct((b, N_HEADS, HEAD_DIM, HEAD_DIM), F32)],
        scratch_shapes=[pltpu.VMEM((N_PAIRS, LANES, LANES), F32),
                        pltpu.VMEM((8 + rows, QKV_WIDTH), F32),
                        pltpu.VMEM((rows, HEADS_WIDTH), F32),
                        pltpu.VMEM((rows, HEADS_WIDTH), F32),
                        pltpu.VMEM((rows, HEADS_WIDTH), F32),
                        pltpu.VMEM((rows, GATE_WIDTH), F32),
                        pltpu.VMEM((rows, GATE_WIDTH), F32),
                        pltpu.VMEM((rows // LANES, 8, LANES), F32),
                        pltpu.VMEM((g, c, LANES), BF16),
                        pltpu.VMEM((g, c, 2 * LANES), F32),
                        pltpu.VMEM((g, c, LANES), BF16),
                        pltpu.VMEM((g, c, LANES), F32),
                        pltpu.VMEM((g, c, LANES), BF16),
                        pltpu.VMEM((g, LANES, LANES), BF16),
                        pltpu.VMEM((g, LANES, LANES), F32),
                        pltpu.VMEM((g, 1, LANES), F32)],
        compiler_params=_params("arbitrary", "arbitrary"),
        name="delta_rule_staged",
    )(ha, hz, hg, conv0, dn0, w_conv, a_log, dt_bias, norm_g, tril, head_ones, pair_ones)


ATTN_WINDOW = BAND_ROWS + PAIR_ROWS


def _attn_pairs_kernel(q_ref, kp_ref, ks_ref, vp_ref, vs_ref, bias_ref, o_ref, kwin_ref, vwin_ref):
    i = pl.program_id(1)
    w, tq = BAND_ROWS, ROW_TILE
    kwin_ref[0:w, :] = kp_ref[...]
    kwin_ref[w:w + tq, :] = ks_ref[...]
    vwin_ref[0:w, :] = vp_ref[...]
    vwin_ref[w:w + tq, :] = vs_ref[...]
    lo = lax.broadcasted_iota(jnp.int32, (1, LANES), 1) < HEAD_DIM
    key_idx = lax.broadcasted_iota(jnp.int32, (1, ATTN_WINDOW), 1)

    def chunk_pair(m, carry):
        r0 = pl.multiple_of(m * PAIR_ROWS, PAIR_ROWS)
        before_start = (i * tq + r0 + key_idx - w) < 0
        for p in range(N_PAIRS):
            ls = slice(p * LANES, (p + 1) * LANES)
            q = q_ref[pl.ds(r0, PAIR_ROWS), ls]
            kk = kwin_ref[pl.ds(r0, ATTN_WINDOW), ls]
            vv = vwin_ref[pl.ds(r0, ATTN_WINDOW), ls]
            zero = jnp.zeros_like(q)
            q2 = jnp.concatenate([jnp.where(lo, q, zero), jnp.where(lo, zero, q)], axis=0)
            s = _dot_nt(q2, kk) * HEAD_DIM ** -0.5 + bias_ref[p]
            s = jnp.where(before_start, -1e30, s)
            pexp = jnp.exp(s - jnp.max(s, axis=-1, keepdims=True))
            denom = jnp.sum(pexp, axis=-1, keepdims=True)
            pv = _dot(pexp.astype(BF16), vv) / denom
            o_ref[pl.ds(r0, PAIR_ROWS), ls] = jnp.where(lo, pv[0:PAIR_ROWS], pv[PAIR_ROWS:2 * PAIR_ROWS])
        return carry

    lax.fori_loop(0, tq // PAIR_ROWS, chunk_pair, 0)


def _pair_bias(rel_table):
    a = jnp.arange(PAIR_ROWS)[:, None]
    idx = jnp.arange(ATTN_WINDOW)[None, :]
    dist = a - (idx - BAND_ROWS)
    first = (a // CHUNK) * CHUNK
    in_band = (idx >= first) & (idx < first + BAND_ROWS + CHUNK)
    bias = rel_table[:, jnp.clip(dist, -REL_CLIP, REL_CLIP) + REL_CLIP].astype(F32)
    bias = jnp.where(in_band[None], bias, -1e30)
    return bias.reshape(N_PAIRS, 2 * PAIR_ROWS, ATTN_WINDOW)


def _attention_pairs(q, k, v, bias):
    b, l, _ = q.shape
    tq = ROW_TILE
    assert tq == BAND_ROWS and l % tq == 0
    own = pl.BlockSpec((None, tq, HEADS_WIDTH), lambda bi, i: (bi, i, 0))
    prev = pl.BlockSpec((None, BAND_ROWS, HEADS_WIDTH), lambda bi, i: (bi, jnp.maximum(i - 1, 0), 0))
    return pl.pallas_call(
        _attn_pairs_kernel,
        grid=(b, l // tq),
        in_specs=[own, prev, own, prev, own,
                  pl.BlockSpec((N_PAIRS, 2 * PAIR_ROWS, ATTN_WINDOW), lambda bi, i: (0, 0, 0))],
        out_specs=own,
        out_shape=jax.ShapeDtypeStruct((b, l, HEADS_WIDTH), F32),
        scratch_shapes=[pltpu.VMEM((BAND_ROWS + tq, HEADS_WIDTH), BF16),
                        pltpu.VMEM((BAND_ROWS + tq, HEADS_WIDTH), BF16)],
        compiler_params=_params("arbitrary", "arbitrary"),
        name="band_attention_pairs",
    )(q, k, k, v, v, bias)


def _mix_kernel(tile, start, full_f32, oa_ref, ob_ref, hc_ref, x_ref, pool0_ref, pw_ref, ps_ref, wout_ref, g_ref, b_ref,
                x1_ref, ext_ref):
    i = pl.program_id(1)
    prec = HIGHEST if full_f32 else None
    wdt = wout_ref.dtype
    hist0 = 16 - POOL_HIST

    @pl.when(i == 0)
    def _():
        ext_ref[0:hist0, :] = jnp.zeros((hist0, POOL_WIDTH), F32)
        ext_ref[hist0:16, :] = pool0_ref[...]

    u = hc_ref[...]
    ext_ref[16:16 + tile, :] = u
    window = 2 << (lax.broadcasted_iota(jnp.int32, (1, POOL_WIDTH), 1) // POOL_GROUP_DIM)
    wsum = u
    for j in range(1, max(POOL_WINDOWS)):
        wsum = wsum + jnp.where(j < window, ext_ref[16 - j:16 - j + tile, :], 0.0)
    ext_ref[hist0:16, :] = u[tile - POOL_HIST:tile, :]
    pos1 = start + i * tile + 1 + lax.broadcasted_iota(jnp.int32, (tile, 1), 0)
    cnt = jnp.minimum(pos1, window).astype(F32)
    res = wsum / cnt - u
    oc = _dot(res, pw_ref[...], prec) * ps_ref[...]

    mix = _dot(oa_ref[...].astype(wdt), wout_ref[0:HEADS_WIDTH, :], prec)
    mix = mix + _dot(ob_ref[...].astype(wdt), wout_ref[HEADS_WIDTH:2 * HEADS_WIDTH, :], prec)
    mix = mix + _dot(oc.astype(wdt), wout_ref[2 * HEADS_WIDTH:D_MODEL, :], prec)
    x1_ref[...] = _layer_norm(DEEPNORM_ALPHA * x_ref[...] + mix, g_ref[...], b_ref[...])


def _mix(oa, ob, hc, x, pool0, pool_wbd, pool_scale, w_out, ln_g, ln_b, start, full_f32):
    b, l, _ = x.shape
    tile = min(ROW_TILE, l)
    row = lambda w: pl.BlockSpec((None, tile, w), lambda bi, i: (bi, i, 0))
    whole = lambda shape: pl.BlockSpec(shape, lambda bi, i: (0,) * len(shape))
    return pl.pallas_call(
        functools.partial(_mix_kernel, tile, start, full_f32),
        grid=(b, l // tile),
        in_specs=[row(HEADS_WIDTH), row(HEADS_WIDTH), row(POOL_WIDTH), row(D_MODEL),
                  pl.BlockSpec((None, POOL_HIST, POOL_WIDTH), lambda bi, i: (bi, 0, 0)),
                  whole((POOL_WIDTH, POOL_WIDTH)), whole((1, POOL_WIDTH)), whole((D_MODEL, D_MODEL)),
                  whole((1, D_MODEL)), whole((1, D_MODEL))],
        out_specs=row(D_MODEL),
        out_shape=jax.ShapeDtypeStruct((b, l, D_MODEL), F32),
        scratch_shapes=[pltpu.VMEM((16 + tile, POOL_WIDTH), F32)],
        compiler_params=_params("arbitrary", "arbitrary"),
        name="pool_outproj_ln1",
    )(oa, ob, hc, x, pool0, pool_wbd, pool_scale, w_out, ln_g, ln_b)


def _top2_of4(a, b, c, d):
    hi1, lo1 = jnp.maximum(a, b), jnp.minimum(a, b)
    hi2, lo2 = jnp.maximum(c, d), jnp.minimum(c, d)
    return jnp.maximum(hi1, hi2), jnp.maximum(jnp.minimum(hi1, hi2), jnp.maximum(lo1, lo2))


def _route(logits_t, bias_t):
    aff = jax.nn.sigmoid(logits_t[0:N_EXPERTS, :])
    sel = aff + bias_t[0:N_EXPERTS, :]
    t = sel.shape[1]
    scores = []
    for gi in range(N_EXPERT_GROUPS):
        r = [sel[4 * gi + m:4 * gi + m + 1, :] for m in range(EXPERTS_PER_GROUP)]
        top1, top2 = _top2_of4(*r)
        scores.append(top1 + top2)
    best = scores[0]
    best_g = jnp.zeros((1, t), jnp.int32)
    for gi in range(1, N_EXPERT_GROUPS):
        better = scores[gi] > best
        best = jnp.where(better, scores[gi], best)
        best_g = jnp.where(better, gi, best_g)
    e_idx = lax.broadcasted_iota(jnp.int32, (N_EXPERTS, t), 0)
    masked = jnp.where(e_idx // EXPERTS_PER_GROUP == best_g, sel, -jnp.inf)
    m1 = jnp.max(masked, axis=0, keepdims=True)
    i1 = jnp.min(jnp.where(masked == m1, e_idx, N_EXPERTS), axis=0, keepdims=True)
    rest = jnp.where(e_idx == i1, -jnp.inf, masked)
    m2 = jnp.max(rest, axis=0, keepdims=True)
    i2 = jnp.min(jnp.where(rest == m2, e_idx, N_EXPERTS), axis=0, keepdims=True)
    w1 = jnp.sum(jnp.where(e_idx == i1, aff, 0.0), axis=0, keepdims=True)
    w2 = jnp.sum(jnp.where(e_idx == i2, aff, 0.0), axis=0, keepdims=True)
    tot = w1 + w2
    gates = jnp.where(e_idx == i1, w1 / tot, 0.0) + jnp.where(e_idx == i2, w2 / tot, 0.0)
    return jnp.concatenate([gates, jnp.zeros((LANES - N_EXPERTS, t), F32)], axis=0)


def _expert(xb, w_gu, w_d, gates, e, prec):
    gu = _dot(xb, w_gu, prec)
    hid = _silu(gu[:, 0:D_EXPERT]) * gu[:, D_EXPERT:2 * D_EXPERT]
    lane = lax.broadcasted_iota(jnp.int32, (1, LANES), 1)
    ge = jnp.sum(jnp.where(lane == e, gates, 0.0), axis=-1, keepdims=True)
    return _dot(hid.astype(w_d.dtype), w_d, prec) * ge


def _ffn_kernel(x_ref, wr_ref, rb_ref, wgu_ref, wd_ref, g_ref, b_ref, y_ref):
    x = x_ref[...]
    xb = x.astype(BF16)
    logits = _dot(xb, wr_ref[...])
    gates = _route(logits.T, rb_ref[...]).T
    acc = jnp.zeros(x.shape, F32)
    for e in range(N_EXPERTS):
        acc = acc + _expert(xb, wgu_ref[e], wd_ref[e], gates, e, None)
    y_ref[...] = _layer_norm(DEEPNORM_ALPHA * x + acc, g_ref[...], b_ref[...])


def _ffn_f32_kernel(x_ref, wr_ref, rb_ref, wgu_ref, wd_ref, g_ref, b_ref, y_ref, gates_ref, acc_ref):
    e = pl.program_id(1)
    x = x_ref[...]

    @pl.when(e == 0)
    def _():
        logits = _dot(x, wr_ref[...], HIGHEST)
        gates_ref[...] = _route(logits.T, rb_ref[...]).T
        acc_ref[...] = jnp.zeros(acc_ref.shape, F32)

    acc_ref[...] += _expert(x, wgu_ref[...], wd_ref[...], gates_ref[...], e, HIGHEST)

    @pl.when(e == N_EXPERTS - 1)
    def _():
        y_ref[...] = _layer_norm(DEEPNORM_ALPHA * x + acc_ref[...], g_ref[...], b_ref[...])


def _ffn(x2d, w_router, router_bias, w_gu, w_down, ln_g, ln_b, full_f32):
    t = x2d.shape[0]
    tm = min(ROW_TILE, t)
    out_shape = jax.ShapeDtypeStruct((t, D_MODEL), F32)
    if full_f32:
        whole = lambda shape: pl.BlockSpec(shape, lambda i, e: (0,) * len(shape))
        return pl.pallas_call(
            _ffn_f32_kernel,
            grid=(t // tm, N_EXPERTS),
            in_specs=[pl.BlockSpec((tm, D_MODEL), lambda i, e: (i, 0)),
                      whole((D_MODEL, LANES)), whole((LANES, 1)),
                      pl.BlockSpec((None, D_MODEL, 2 * D_EXPERT), lambda i, e: (e, 0, 0)),
                      pl.BlockSpec((None, D_EXPERT, D_MODEL), lambda i, e: (e, 0, 0)),
                      whole((1, D_MODEL)), whole((1, D_MODEL))],
            out_specs=pl.BlockSpec((tm, D_MODEL), lambda i, e: (i, 0)),
            out_shape=out_shape,
            scratch_shapes=[pltpu.VMEM((tm, LANES), F32), pltpu.VMEM((tm, D_MODEL), F32)],
            compiler_params=_params("arbitrary", "arbitrary"),
            name="routed_ffn_ln2_f32",
        )(x2d, w_router, router_bias, w_gu, w_down, ln_g, ln_b)
    whole = lambda shape: pl.BlockSpec(shape, lambda i: (0,) * len(shape), pipeline_mode=pl.Buffered(1))
    return pl.pallas_call(
        _ffn_kernel,
        grid=(t // tm,),
        in_specs=[pl.BlockSpec((tm, D_MODEL), lambda i: (i, 0)),
                  whole((D_MODEL, LANES)), whole((LANES, 1)),
                  whole((N_EXPERTS, D_MODEL, 2 * D_EXPERT)), whole((N_EXPERTS, D_EXPERT, D_MODEL)),
                  whole((1, D_MODEL)), whole((1, D_MODEL))],
        out_specs=pl.BlockSpec((tm, D_MODEL), lambda i: (i, 0)),
        out_shape=out_shape,
        compiler_params=_params("arbitrary"),
        name="routed_ffn_ln2",
    )(x2d, w_router, router_bias, w_gu, w_down, ln_g, ln_b)


def _prep_layer(l, wdt, w_in, w_conv, a_log, dt_bias, dn_norm_g, rel_table, pool_w, pool_scale, w_out,
                ln1_g, ln1_b, w_gate, w_up, w_down, ln2_g, ln2_b):
    wi = w_in[l]
    gates = jnp.zeros((D_MODEL, GATE_WIDTH), F32).at[:, 0:2 * N_HEADS].set(wi[:, OFF_AA:OFF_BQ])
    w_perm = jnp.concatenate([wi[:, OFF_AQ:OFF_AZ], wi[:, OFF_AZ:OFF_AA], wi[:, OFF_BQ:OFF_CU],
                              wi[:, OFF_CU:IN_WIDTH], gates], axis=1).astype(wdt)
    pad_heads = lambda vec: jnp.zeros((1, GATE_WIDTH), F32).at[0, 0:N_HEADS].set(vec)
    pool_wbd = jnp.zeros((POOL_WIDTH, POOL_WIDTH), F32)
    for gi in range(len(POOL_WINDOWS)):
        s = slice(gi * POOL_GROUP_DIM, (gi + 1) * POOL_GROUP_DIM)
        pool_wbd = pool_wbd.at[s, s].set(pool_w[l, gi])
    return dict(
        w_perm=w_perm, w_conv=w_conv[l], a_log=pad_heads(a_log[l]), dt_bias=pad_heads(dt_bias[l]),
        norm_g=jnp.tile(dn_norm_g[l], N_HEADS)[None, :], rel_table=rel_table[l],
        pool_wbd=pool_wbd, pool_scale=pool_scale[l][None, :], w_out=w_out[l].astype(wdt),
        ln1_g=ln1_g[l][None, :], ln1_b=ln1_b[l][None, :],
        w_gu=jnp.concatenate([w_gate[l], w_up[l]], axis=-1).astype(wdt), w_down=w_down[l].astype(wdt),
        ln2_g=ln2_g[l][None, :], ln2_b=ln2_b[l][None, :])


def _rel_bias(rel_table, c):
    dist = jnp.arange(c)[:, None] - (jnp.arange(BAND_ROWS + c)[None, :] - BAND_ROWS)
    return rel_table[:, jnp.clip(dist, -REL_CLIP, REL_CLIP) + REL_CLIP].astype(F32)


def _last_rows(t, n):
    if t.shape[1] < n:
        t = jnp.concatenate([jnp.zeros((t.shape[0], n - t.shape[1]) + t.shape[2:], t.dtype), t], axis=1)
    return t[:, t.shape[1] - n:]


def _trunk(x, start, dn_state, conv_state, k_hist, v_hist, pool_state, band_rows, layers, w_router, router_bias,
           full_f32):
    b, l, _ = x.shape
    assert l >= POOL_HIST
    c = min(CHUNK, l)
    new_dn, new_conv, new_k, new_v, new_pool = [], [], [], [], []
    for li, p in enumerate(layers):
        ha, hz, hb, hc, hg = _inproj(x.reshape(b * l, D_MODEL), p["w_perm"], full_f32)
        ha = ha.reshape(b, l, QKV_WIDTH)
        hz = hz.reshape(b, l, HEADS_WIDTH)
        hb = hb.reshape(b, l, QKV_WIDTH)
        hc = hc.reshape(b, l, POOL_WIDTH)
        hg = hg.reshape(b, l, GATE_WIDTH)
        oa, dn_new = _delta(ha, hz, hg, conv_state[li], dn_state[li], p["w_conv"], p["a_log"], p["dt_bias"],
                            p["norm_g"], full_f32)
        qb, kb, vb = (hb[..., j * HEADS_WIDTH:(j + 1) * HEADS_WIDTH] for j in range(3))
        bias = _rel_bias(p["rel_table"], c)
        if k_hist is None:
            ob = _attention(qb, kb, vb, None, None, bias, full_f32)
            k_new, v_new = _last_rows(kb, band_rows), _last_rows(vb, band_rows)
        else:
            kh = k_hist[li].reshape(b, -1, HEADS_WIDTH)
            vh = v_hist[li].reshape(b, -1, HEADS_WIDTH)
            ob = _attention(qb, kb, vb, kh, vh, bias, full_f32)
            k_new = _last_rows(jnp.concatenate([kh, kb], axis=1), band_rows)
            v_new = _last_rows(jnp.concatenate([vh, vb], axis=1), band_rows)
        x1 = _mix(oa, ob, hc, x, pool_state[li], p["pool_wbd"], p["pool_scale"], p["w_out"], p["ln1_g"],
                  p["ln1_b"], start, full_f32)
        x = _ffn(x1.reshape(b * l, D_MODEL), w_router.astype(p["w_out"].dtype), router_bias, p["w_gu"],
                 p["w_down"], p["ln2_g"], p["ln2_b"], full_f32).reshape(b, l, D_MODEL)
        new_dn.append(dn_new)
        new_conv.append(ha[:, l - (DN_CONV - 1):])
        new_k.append(k_new.reshape(b, band_rows, N_HEADS, HEAD_DIM))
        new_v.append(v_new.reshape(b, band_rows, N_HEADS, HEAD_DIM))
        new_pool.append(hc[:, l - POOL_HIST:])
    return (x, jnp.stack(new_dn), jnp.stack(new_conv), jnp.stack(new_k), jnp.stack(new_v), jnp.stack(new_pool))


def _trunk_seq(x, dn_state, conv_state, pool_state, band_rows, layers, w_router, router_bias):
    b, l, _ = x.shape
    assert band_rows == BAND_ROWS and l >= band_rows
    new_dn, new_conv, new_k, new_v, new_pool = [], [], [], [], []
    for li, p in enumerate(layers):
        ha, hz, hb, hc, hg, kv_last = _inproj_seq(x, p["w_perm"])
        oa, dn_new = _delta_staged(ha, hz, hg, conv_state[li], dn_state[li], p["w_conv"], p["a_log"], p["dt_bias"],
                                   p["norm_g"])
        qb, kb, vb = (hb[..., j * HEADS_WIDTH:(j + 1) * HEADS_WIDTH] for j in range(3))
        ob = _attention_pairs(qb, kb, vb, _pair_bias(p["rel_table"]))
        x1 = _mix(oa, ob, hc, x, pool_state[li], p["pool_wbd"], p["pool_scale"], p["w_out"], p["ln1_g"],
                  p["ln1_b"], 0, False)
        x = _ffn(x1.reshape(b * l, D_MODEL), w_router.astype(BF16), router_bias, p["w_gu"], p["w_down"],
                 p["ln2_g"], p["ln2_b"], False).reshape(b, l, D_MODEL)
        new_dn.append(dn_new)
        new_conv.append(ha[:, l - (DN_CONV - 1):])
        new_k.append(kv_last[..., 0:HEADS_WIDTH].reshape(b, band_rows, N_HEADS, HEAD_DIM))
        new_v.append(kv_last[..., HEADS_WIDTH:].reshape(b, band_rows, N_HEADS, HEAD_DIM))
        new_pool.append(hc[:, l - POOL_HIST:])
    return (x, jnp.stack(new_dn), jnp.stack(new_conv), jnp.stack(new_k), jnp.stack(new_v), jnp.stack(new_pool))


@jax.jit
def kernel(x_prompt, x_sample, state_dn, state_conv, cache_k, cache_v, state_pool, w_in, w_conv, a_log, dt_bias,
           dn_norm_g, rel_table, pool_w, pool_scale, w_out, ln1_g, ln1_b, w_router, router_bias, w_gate, w_up,
           w_down, ln2_g, ln2_b):
    weights = (w_in, w_conv, a_log, dt_bias, dn_norm_g, rel_table, pool_w, pool_scale, w_out,
               ln1_g, ln1_b, w_gate, w_up, w_down, ln2_g, ln2_b)
    layers_bf16 = [_prep_layer(l, BF16, *weights) for l in range(DEPTH)]
    layers_f32 = [_prep_layer(l, F32, *weights) for l in range(DEPTH)]
    w_router_p = jnp.zeros((D_MODEL, LANES), F32).at[:, 0:N_EXPERTS].set(w_router)
    router_bias_p = jnp.zeros((LANES, 1), F32).at[0:N_EXPERTS, 0].set(router_bias)
    band_rows = cache_k.shape[2]
    bp = x_prompt.shape[0]
    zero_dn = jnp.zeros((DEPTH, bp, N_HEADS, HEAD_DIM, HEAD_DIM), state_dn.dtype)
    zero_conv = jnp.zeros((DEPTH, bp, DN_CONV - 1, QKV_WIDTH), x_prompt.dtype)
    zero_pool = jnp.zeros((DEPTH, bp, POOL_HIST, POOL_WIDTH), x_prompt.dtype)
    prompt = _trunk_seq(x_prompt, zero_dn, zero_conv, zero_pool, band_rows, layers_bf16, w_router_p, router_bias_p)
    sample = _trunk(x_sample, PAST_LEN, state_dn, state_conv, cache_k, cache_v, state_pool, band_rows, layers_f32,
                    w_router_p, router_bias_p, True)
    return (prompt[0], sample[0]) + prompt[1:] + sample[1:]
```

```python
import functools
import math

import jax
import jax.numpy as jnp
from jax import lax
from jax.experimental import pallas as pl
from jax.experimental.pallas import tpu as pltpu

F32 = jnp.float32
BF16 = jnp.bfloat16
HIGHEST = lax.Precision.HIGHEST

D_MODEL = 1024
HEAD_DIM = 64
N_HEADS = 6
HEADS_WIDTH = N_HEADS * HEAD_DIM
QKV_WIDTH = 3 * HEADS_WIDTH
DN_CONV = 4
CHUNK = 64
BAND_ROWS = 512
REL_CLIP = 128
POOL_WINDOWS = (2, 4, 8, 16)
POOL_GROUP_DIM = 64
POOL_WIDTH = 256
POOL_HIST = 15
N_EXPERTS = 16
N_EXPERT_GROUPS = 4
EXPERTS_PER_GROUP = 4
D_EXPERT = 256
DEPTH = 2
DEEPNORM_ALPHA = (2 * DEPTH) ** 0.25
LN_EPS = 1e-5
RMS_EPS = 1e-6
PAST_LEN = 1024

OFF_AQ = 0
OFF_AZ = 3 * HEADS_WIDTH
OFF_AA = OFF_AZ + HEADS_WIDTH
OFF_AB = OFF_AA + N_HEADS
OFF_BQ = OFF_AB + N_HEADS
OFF_CU = OFF_BQ + 3 * HEADS_WIDTH
IN_WIDTH = OFF_CU + POOL_WIDTH

LANES = 128
GATE_WIDTH = LANES
SEG_A = (0, QKV_WIDTH)
SEG_Z = (SEG_A[1], SEG_A[1] + HEADS_WIDTH)
SEG_B = (SEG_Z[1], SEG_Z[1] + QKV_WIDTH)
SEG_C = (SEG_B[1], SEG_B[1] + POOL_WIDTH)
SEG_G = (SEG_C[1], SEG_C[1] + GATE_WIDTH)
PERM_WIDTH = SEG_G[1]

ROW_TILE = 512
VMEM_LIMIT = 56 * 1024 * 1024


def _params(*sem):
    return pltpu.CompilerParams(dimension_semantics=sem, vmem_limit_bytes=VMEM_LIMIT)


def _dot(a, b, precision=None):
    return jnp.dot(a, b, preferred_element_type=F32, precision=precision)


def _dot_nt(a, b, precision=None):
    return lax.dot_general(a, b, (((1,), (1,)), ((), ())), preferred_element_type=F32, precision=precision)


def _dot_tn(a, b, precision=None):
    return lax.dot_general(a, b, (((0,), (0,)), ((), ())), preferred_element_type=F32, precision=precision)


def _silu(x):
    return x * jax.nn.sigmoid(x)


def _layer_norm(x, g, b):
    mu = jnp.mean(x, axis=-1, keepdims=True)
    xc = x - mu
    var = jnp.mean(xc * xc, axis=-1, keepdims=True)
    return xc * lax.rsqrt(var + LN_EPS) * g + b


def _inproj_kernel(full_f32, x_ref, w_ref, ha_ref, hz_ref, hb_ref, hc_ref, hg_ref):
    prec = HIGHEST if full_f32 else None
    xb = x_ref[...].astype(w_ref.dtype)
    for (lo, hi), out_ref in ((SEG_A, ha_ref), (SEG_Z, hz_ref), (SEG_B, hb_ref), (SEG_C, hc_ref), (SEG_G, hg_ref)):
        step = 384 if (hi - lo) % 384 == 0 else hi - lo
        for c0 in range(0, hi - lo, step):
            out_ref[:, c0:c0 + step] = _dot(xb, w_ref[:, lo + c0:lo + c0 + step], prec)


def _inproj(x2d, w_perm, full_f32):
    t = x2d.shape[0]
    tm = min(ROW_TILE, t)
    widths = [s[1] - s[0] for s in (SEG_A, SEG_Z, SEG_B, SEG_C, SEG_G)]
    return pl.pallas_call(
        functools.partial(_inproj_kernel, full_f32),
        grid=(t // tm,),
        in_specs=[pl.BlockSpec((tm, D_MODEL), lambda i: (i, 0)),
                  pl.BlockSpec((D_MODEL, PERM_WIDTH), lambda i: (0, 0))],
        out_specs=[pl.BlockSpec((tm, w), lambda i: (i, 0)) for w in widths],
        out_shape=[jax.ShapeDtypeStruct((t, w), F32) for w in widths],
        compiler_params=_params("arbitrary"),
        name="inproj",
    )(x2d, w_perm)


INPROJ_COLS = 512


def _inproj_seq_kernel(x_ref, w_ref, ha_ref, hz_ref, hb_ref, hc_ref, hg_ref, kv_ref):
    xb = x_ref[...].astype(BF16)
    segments = ((SEG_A, ha_ref), (SEG_Z, hz_ref), (SEG_B, hb_ref), (SEG_C, hc_ref), (SEG_G, hg_ref))
    kv0 = SEG_B[0] + HEADS_WIDTH
    for c0 in range(0, PERM_WIDTH, INPROJ_COLS):
        h = _dot(xb, w_ref[:, c0:c0 + INPROJ_COLS])
        for (lo, hi), out_ref in segments:
            a, b = max(lo, c0), min(hi, c0 + INPROJ_COLS)
            if a < b:
                out_ref[:, a - lo:b - lo] = h[:, a - c0:b - c0].astype(out_ref.dtype)
        a, b = max(kv0, c0), min(SEG_B[1], c0 + INPROJ_COLS)
        if a < b:
            kv_ref[:, a - kv0:b - kv0] = h[:, a - c0:b - c0]


def _inproj_seq(x, w_perm):
    b, l, _ = x.shape
    tm = ROW_TILE
    assert tm == BAND_ROWS and l % tm == 0
    widths = [s[1] - s[0] for s in (SEG_A, SEG_Z, SEG_B, SEG_C, SEG_G)]
    dtypes = [F32, F32, BF16, F32, F32]
    row = lambda w: pl.BlockSpec((None, tm, w), lambda bi, i: (bi, i, 0))
    return pl.pallas_call(
        _inproj_seq_kernel,
        grid=(b, l // tm),
        in_specs=[row(D_MODEL), pl.BlockSpec((D_MODEL, PERM_WIDTH), lambda bi, i: (0, 0))],
        out_specs=[row(w) for w in widths] + [pl.BlockSpec((None, tm, 2 * HEADS_WIDTH), lambda bi, i: (bi, 0, 0))],
        out_shape=[jax.ShapeDtypeStruct((b, l, w), d) for w, d in zip(widths, dtypes)]
        + [jax.ShapeDtypeStruct((b, tm, 2 * HEADS_WIDTH), F32)],
        compiler_params=_params("arbitrary", "arbitrary"),
        name="inproj_seq",
    )(x, w_perm)


def _delta_kernel(c, full_f32, ha_ref, hz_ref, hg_ref, conv0_ref, dn0_ref, wconv_ref, alog_ref, dtb_ref, ng_ref,
                  oa_ref, dn_ref, s_ref, ext_ref):
    n = pl.program_id(1)
    prec = HIGHEST if full_f32 else None
    pad = 8 - (DN_CONV - 1)

    @pl.when(n == 0)
    def _():
        s_ref[...] = dn0_ref[...]
        ext_ref[pad:8, :] = conv0_ref[...]

    u = ha_ref[...]
    ext_ref[8:8 + c, :] = u
    conv = ext_ref[pad:pad + c, :] * wconv_ref[0:1, :]
    for j in range(1, DN_CONV):
        conv = conv + ext_ref[pad + j:pad + j + c, :] * wconv_ref[j:j + 1, :]
    ext_ref[pad:8, :] = u[c - (DN_CONV - 1):c, :]
    qkv = _silu(conv)
    q = qkv[:, 0:HEADS_WIDTH]
    k = qkv[:, HEADS_WIDTH:2 * HEADS_WIDTH]
    v = qkv[:, 2 * HEADS_WIDTH:QKV_WIDTH]

    r = lax.broadcasted_iota(jnp.int32, (HEADS_WIDTH, HEADS_WIDTH), 0) // HEAD_DIM
    cc = lax.broadcasted_iota(jnp.int32, (HEADS_WIDTH, HEADS_WIDTH), 1) // HEAD_DIM
    head_ones = (r == cc).astype(F32)
    qn = q * lax.rsqrt(_dot(q * q, head_ones, HIGHEST) + RMS_EPS) * HEAD_DIM ** -0.5
    kn = k * lax.rsqrt(_dot(k * k, head_ones, HIGHEST) + RMS_EPS)

    hg = hg_ref[...]
    lane = lax.broadcasted_iota(jnp.int32, (1, GATE_WIDTH), 1)
    neg_rate = jnp.where(lane < N_HEADS, -jnp.exp(alog_ref[...]), 0.0)
    g = neg_rate * jax.nn.softplus(hg + dtb_ref[...])
    beta = jax.nn.sigmoid(hg)
    ri = lax.broadcasted_iota(jnp.int32, (c, c), 0)
    ci = lax.broadcasted_iota(jnp.int32, (c, c), 1)
    incl = ri >= ci
    strict = ri > ci
    gc = _dot(incl.astype(F32), g, HIGHEST)
    gc_t = _dot_tn(g, (ri <= ci).astype(F32), HIGHEST)

    z = hz_ref[...]
    for h in range(N_HEADS):
        sl = slice(h * HEAD_DIM, (h + 1) * HEAD_DIM)
        qh, kh, vh = qn[:, sl], kn[:, sl], v[:, sl]
        gch = gc[:, h:h + 1]
        diff = gch - gc_t[h:h + 1, :]
        decay = jnp.where(incl, jnp.exp(jnp.where(incl, diff, 0.0)), 0.0)
        bh = beta[:, N_HEADS + h:N_HEADS + h + 1]
        kbeta = kh * bh
        eg = jnp.exp(gch)
        neg_m = jnp.where(strict, -(_dot_nt(kbeta, kh, prec) * decay), 0.0)
        sol = jnp.concatenate([vh * bh, kbeta * eg], axis=-1)
        power = neg_m
        sol = sol + _dot(power, sol, prec)
        for _ in range(int(math.log2(c)) - 1):
            power = _dot(power, power, prec)
            sol = sol + _dot(power, sol, prec)
        u_h, w_h = sol[:, 0:HEAD_DIM], sol[:, HEAD_DIM:2 * HEAD_DIM]
        a_intra = jnp.where(incl, _dot_nt(qh, kh, prec) * decay, 0.0)
        g_last = gc[c - 1:c, h:h + 1]
        k_tail = kh * jnp.exp(g_last - gch)
        s = s_ref[h]
        v_new = u_h - _dot(w_h, s, prec)
        o = _dot(qh * eg, s, prec) + _dot(a_intra, v_new, prec)
        s_ref[h] = s * jnp.exp(g_last) + _dot_tn(k_tail, v_new, prec)
        o = o * lax.rsqrt(jnp.mean(o * o, axis=-1, keepdims=True) + RMS_EPS)
        oa_ref[:, sl] = o * ng_ref[:, sl] * _silu(z[:, sl])

    @pl.when(n == pl.num_programs(1) - 1)
    def _():
        dn_ref[...] = s_ref[...]


def _delta(ha, hz, hg, conv0, dn0, w_conv, a_log, dt_bias, norm_g, full_f32):
    b, l, _ = ha.shape
    c = min(CHUNK, l)
    row = lambda w: pl.BlockSpec((None, c, w), lambda i, n: (i, n, 0))
    whole = lambda shape: pl.BlockSpec(shape, lambda i, n: (0,) * len(shape))
    state = pl.BlockSpec((None, N_HEADS, HEAD_DIM, HEAD_DIM), lambda i, n: (i, 0, 0, 0))
    return pl.pallas_call(
        functools.partial(_delta_kernel, c, full_f32),
        grid=(b, l // c),
        in_specs=[row(QKV_WIDTH), row(HEADS_WIDTH), row(GATE_WIDTH),
                  pl.BlockSpec((None, DN_CONV - 1, QKV_WIDTH), lambda i, n: (i, 0, 0)), state,
                  whole((DN_CONV, QKV_WIDTH)), whole((1, GATE_WIDTH)), whole((1, GATE_WIDTH)),
                  whole((1, HEADS_WIDTH))],
        out_specs=[row(HEADS_WIDTH), state],
        out_shape=[jax.ShapeDtypeStruct((b, l, HEADS_WIDTH), F32),
                   jax.ShapeDtypeStruct((b, N_HEADS, HEAD_DIM, HEAD_DIM), F32)],
        scratch_shapes=[pltpu.VMEM((N_HEADS, HEAD_DIM, HEAD_DIM), F32),
                        pltpu.VMEM((8 + c, QKV_WIDTH), F32)],
        compiler_params=_params("arbitrary", "arbitrary"),
        name="delta_rule",
    )(ha, hz, hg, conv0, dn0, w_conv, a_log, dt_bias, norm_g)


PAIR_ROWS = 2 * CHUNK
N_PAIRS = N_HEADS // 2


def _delta_pairs_kernel(ha_ref, hz_ref, hg_ref, conv0_ref, dn0_ref, wconv_ref, alog_ref, dtb_ref, ng_ref,
                        oa_ref, dn_ref, z_ref, ext_ref):
    n = pl.program_id(1)
    c, rows = CHUNK, PAIR_ROWS
    pad = 8 - (DN_CONV - 1)
    lane = lax.broadcasted_iota(jnp.int32, (1, LANES), 1)
    lo = lane < HEAD_DIM
    zero64 = jnp.zeros((HEAD_DIM, HEAD_DIM), F32)

    @pl.when(n == 0)
    def _():
        for p in range(N_PAIRS):
            top = jnp.concatenate([zero64, dn0_ref[2 * p + 1]], axis=1)
            bot = jnp.concatenate([dn0_ref[2 * p], zero64], axis=1)
            z_ref[p] = jnp.concatenate([top, bot], axis=0)
        ext_ref[pad:8, :] = conv0_ref[...]

    u = ha_ref[...]
    ext_ref[8:8 + rows, :] = u
    conv = ext_ref[pad:pad + rows, :] * wconv_ref[0:1, :]
    for j in range(1, DN_CONV):
        conv = conv + ext_ref[pad + j:pad + j + rows, :] * wconv_ref[j:j + 1, :]
    ext_ref[pad:8, :] = u[rows - (DN_CONV - 1):rows, :]
    qkv = _silu(conv)
    q = qkv[:, 0:HEADS_WIDTH]
    k = qkv[:, HEADS_WIDTH:2 * HEADS_WIDTH]
    v = qkv[:, 2 * HEADS_WIDTH:QKV_WIDTH]
    hr = lax.broadcasted_iota(jnp.int32, (HEADS_WIDTH, HEADS_WIDTH), 0) // HEAD_DIM
    hc = lax.broadcasted_iota(jnp.int32, (HEADS_WIDTH, HEADS_WIDTH), 1) // HEAD_DIM
    head_ones = (hr == hc).astype(BF16)
    qn = q * lax.rsqrt(_dot((q * q).astype(BF16), head_ones) + RMS_EPS) * HEAD_DIM ** -0.5
    kn = k * lax.rsqrt(_dot((k * k).astype(BF16), head_ones) + RMS_EPS)
    pr = lax.broadcasted_iota(jnp.int32, (LANES, LANES), 0) // HEAD_DIM
    pc = lax.broadcasted_iota(jnp.int32, (LANES, LANES), 1) // HEAD_DIM
    pair_ones = (pr == pc).astype(BF16)
    anti = pr != pc

    hg = hg_ref[...]
    neg_rate = jnp.where(lane < N_HEADS, -jnp.exp(alog_ref[...]), 0.0)
    g = neg_rate * jax.nn.softplus(hg + dtb_ref[...])
    beta = jax.nn.sigmoid(hg)
    ri = lax.broadcasted_iota(jnp.int32, (rows, rows), 0)
    ci = lax.broadcasted_iota(jnp.int32, (rows, rows), 1)
    gc = _dot(((ri >= ci) & (ri // c == ci // c)).astype(F32), g, HIGHEST)
    gc_t = gc.T[0:8, :]
    gc_t_sw = pltpu.roll(gc_t, HEAD_DIM, axis=1)

    row_i = lax.broadcasted_iota(jnp.int32, (c, 1), 0)
    col_j = lane % HEAD_DIM
    incl = row_i >= col_j
    strict = row_i > col_j
    z = hz_ref[...]
    swap = lambda x: jnp.concatenate([x[HEAD_DIM:], x[:HEAD_DIM]], axis=0)
    zeros_cl = jnp.zeros((c, LANES), F32)

    for p in range(N_PAIRS):
        e, o = 2 * p, 2 * p + 1
        ls = slice(p * LANES, (p + 1) * LANES)
        zst = z_ref[p]
        for s in range(rows // c):
            rs = slice(s * c, (s + 1) * c)
            kp, vp, qp = kn[rs, ls], v[rs, ls], qn[rs, ls]
            ksw = pltpu.roll(kp, HEAD_DIM, axis=1)
            qsw = pltpu.roll(qp, HEAD_DIM, axis=1)
            g_e, g_o = gc[rs, e:e + 1], gc[rs, o:o + 1]
            b_e, b_o = beta[rs, N_HEADS + e:N_HEADS + e + 1], beta[rs, N_HEADS + o:N_HEADS + o + 1]
            gl_e, gl_o = g_e[c - 1:c, :], g_o[c - 1:c, :]
            eg_e, eg_o = jnp.exp(g_e), jnp.exp(g_o)
            if s == 0:
                g_row = jnp.where(lo, gc_t[e:e + 1, :], gc_t_sw[o:o + 1, :])
            else:
                g_row = jnp.where(lo, gc_t_sw[e:e + 1, :], gc_t[o:o + 1, :])
            diff = jnp.where(lo, g_e, g_o) - g_row
            decay = jnp.where(incl, jnp.exp(jnp.where(incl, diff, 0.0)), 0.0)
            kb = kp * jnp.where(lo, b_e, b_o)
            k_blocks = jnp.concatenate([jnp.where(lo, kp, 0.0), jnp.where(lo, 0.0, kp)], axis=0).astype(BF16)
            st = _dot_nt(jnp.concatenate([kb, qp], axis=0).astype(BF16), k_blocks)
            power = jnp.where(strict, -(st[0:c] * decay), 0.0)
            a2 = jnp.where(incl, st[c:2 * c] * decay, 0.0)
            sol_e = jnp.where(lo, vp, ksw * eg_e) * b_e
            sol_o = jnp.where(lo, ksw * eg_o, vp) * b_o
            n_stage = int(math.log2(c))
            for stage in range(n_stage):
                tail = stage == n_stage - 1
                top = [sol_e, zeros_cl] + ([] if tail else [jnp.where(lo, power, 0.0)])
                bot = [zeros_cl, sol_o] + ([] if tail else [jnp.where(lo, 0.0, power)])
                rhs = jnp.concatenate([jnp.concatenate(top, axis=1), jnp.concatenate(bot, axis=1)], axis=0)
                res = _dot(power.astype(BF16), rhs.astype(BF16))
                sol_e = sol_e + res[:, 0:LANES]
                sol_o = sol_o + res[:, LANES:2 * LANES]
                if not tail:
                    power = res[:, 2 * LANES:3 * LANES]
            top = jnp.concatenate([jnp.where(lo, sol_e, 0.0), jnp.where(lo, 0.0, sol_e)], axis=1)
            bot = jnp.concatenate([jnp.where(lo, 0.0, sol_o), jnp.where(lo, sol_o, 0.0)], axis=1)
            y = _dot(a2.astype(BF16), jnp.concatenate([top, bot], axis=0).astype(BF16))
            qe_sw = qsw * jnp.where(lo, eg_o, eg_e) - y[:, LANES:2 * LANES]
            kt = kp * jnp.exp(jnp.where(lo, gl_e - g_e, gl_o - g_o))
            kt_blocks = jnp.concatenate([jnp.where(lo, kt, 0.0), jnp.where(lo, 0.0, kt)], axis=0).astype(BF16)
            kbm = _dot_tn(kt_blocks, jnp.concatenate([sol_e, sol_o], axis=0).astype(BF16))
            zb = zst.astype(BF16)
            out = y[:, 0:LANES] + _dot(qe_sw.astype(BF16), zb)
            ks = _dot(jnp.where(anti, kbm, 0.0).astype(BF16), zb)
            zst = zst * jnp.exp(jnp.where(lo, gl_e, gl_o)) - swap(ks) + swap(jnp.where(anti, 0.0, kbm))
            ms = _dot((out * out).astype(BF16), pair_ones) * (1.0 / HEAD_DIM)
            oa_ref[rs, ls] = out * lax.rsqrt(ms + RMS_EPS) * ng_ref[:, ls] * _silu(z[rs, ls])
        z_ref[p] = zst

    @pl.when(n == pl.num_programs(1) - 1)
    def _():
        for p in range(N_PAIRS):
            dn_ref[2 * p] = z_ref[p][HEAD_DIM:, 0:HEAD_DIM]
            dn_ref[2 * p + 1] = z_ref[p][0:HEAD_DIM, HEAD_DIM:]


def _delta_pairs(ha, hz, hg, conv0, dn0, w_conv, a_log, dt_bias, norm_g):
    b, l, _ = ha.shape
    rows = PAIR_ROWS
    row = lambda w: pl.BlockSpec((None, rows, w), lambda i, n: (i, n, 0))
    whole = lambda shape: pl.BlockSpec(shape, lambda i, n: (0,) * len(shape))
    state = pl.BlockSpec((None, N_HEADS, HEAD_DIM, HEAD_DIM), lambda i, n: (i, 0, 0, 0))
    return pl.pallas_call(
        _delta_pairs_kernel,
        grid=(b, l // rows),
        in_specs=[row(QKV_WIDTH), row(HEADS_WIDTH), row(GATE_WIDTH),
                  pl.BlockSpec((None, DN_CONV - 1, QKV_WIDTH), lambda i, n: (i, 0, 0)), state,
                  whole((DN_CONV, QKV_WIDTH)), whole((1, GATE_WIDTH)), whole((1, GATE_WIDTH)),
                  whole((1, HEADS_WIDTH))],
        out_specs=[row(HEADS_WIDTH), state],
        out_shape=[jax.ShapeDtypeStruct((b, l, HEADS_WIDTH), F32),
                   jax.ShapeDtypeStruct((b, N_HEADS, HEAD_DIM, HEAD_DIM), F32)],
        scratch_shapes=[pltpu.VMEM((N_PAIRS, LANES, LANES), F32),
                        pltpu.VMEM((8 + rows, QKV_WIDTH), F32)],
        compiler_params=_params("arbitrary", "arbitrary"),
        name="delta_rule_pairs",
    )(ha, hz, hg, conv0, dn0, w_conv, a_log, dt_bias, norm_g)


def _attn_kernel(c, tq, mask_before_start, full_f32, q_ref, kp_ref, ks_ref, vp_ref, vs_ref, bias_ref, o_ref,
                 kwin_ref, vwin_ref):
    i = pl.program_id(1)
    prec = HIGHEST if full_f32 else None
    w = BAND_ROWS
    kwin_ref[0:w, :] = kp_ref[...]
    kwin_ref[w:w + tq, :] = ks_ref[...]
    vwin_ref[0:w, :] = vp_ref[...]
    vwin_ref[w:w + tq, :] = vs_ref[...]
    key_idx = lax.broadcasted_iota(jnp.int32, (1, w + c), 1)

    def chunk(j, carry):
        r0 = pl.multiple_of(j * c, c)
        q = q_ref[pl.ds(r0, c), :]
        kk = kwin_ref[pl.ds(r0, w + c), :]
        vv = vwin_ref[pl.ds(r0, w + c), :]
        valid = (i * tq + r0 + key_idx - w) >= 0
        for h in range(N_HEADS):
            sl = slice(h * HEAD_DIM, (h + 1) * HEAD_DIM)
            s = _dot_nt(q[:, sl], kk[:, sl], prec) * HEAD_DIM ** -0.5 + bias_ref[h]
            if mask_before_start:
                s = jnp.where(valid, s, -1e30)
            s = s - jnp.max(s, axis=-1, keepdims=True)
            p = jnp.exp(s)
            p = p / jnp.sum(p, axis=-1, keepdims=True)
            o_ref[pl.ds(r0, c), sl] = _dot(p, vv[:, sl], prec)
        return carry

    lax.fori_loop(0, tq // c, chunk, 0)


def _attention(q, k, v, k_hist, v_hist, bias, full_f32):
    b, l, _ = q.shape
    c = min(CHUNK, l)
    tq = min(ROW_TILE, l)
    assert tq == BAND_ROWS or l == tq
    own = pl.BlockSpec((None, tq, HEADS_WIDTH), lambda bi, i: (bi, i, 0))
    if k_hist is None:
        prev = pl.BlockSpec((None, BAND_ROWS, HEADS_WIDTH), lambda bi, i: (bi, jnp.maximum(i - 1, 0), 0))
        k_prev, v_prev = k, v
    else:
        prev = pl.BlockSpec((None, BAND_ROWS, HEADS_WIDTH), lambda bi, i: (bi, 0, 0))
        k_prev, v_prev = k_hist, v_hist
    return pl.pallas_call(
        functools.partial(_attn_kernel, c, tq, k_hist is None, full_f32),
        grid=(b, l // tq),
        in_specs=[own, prev, own, prev, own,
                  pl.BlockSpec((N_HEADS, c, BAND_ROWS + c), lambda bi, i: (0, 0, 0))],
        out_specs=own,
        out_shape=jax.ShapeDtypeStruct((b, l, HEADS_WIDTH), F32),
        scratch_shapes=[pltpu.VMEM((BAND_ROWS + tq, HEADS_WIDTH), F32),
                        pltpu.VMEM((BAND_ROWS + tq, HEADS_WIDTH), F32)],
        compiler_params=_params("arbitrary", "arbitrary"),
        name="band_attention",
    )(q, k_prev, k, v_prev, v, bias)


STAGED_ROWS = 256
STAGED_CHUNKS = STAGED_ROWS // CHUNK
STAGED_PROBLEMS = STAGED_CHUNKS * N_PAIRS


def _delta_staged_kernel(ha_ref, hz_ref, hg_ref, conv0_ref, dn0_ref, wconv_ref, alog_ref, dtb_ref, ng_ref,
                         tril_ref, hones_ref, pones_ref,
                         oa_ref, dn_ref,
                         z_ref, ext_ref, qn_ref, kn_ref, v_ref, gc_ref, beta_ref, gct_ref,
                         pw_ref, sol_ref, a2_ref, ob_ref, qe_ref, kmat_ref, bsw_ref, egl_ref):
    n = pl.program_id(1)
    c, rows = CHUNK, STAGED_ROWS
    pad = 8 - (DN_CONV - 1)
    lane = lax.broadcasted_iota(jnp.int32, (1, LANES), 1)
    lo = lane < HEAD_DIM
    zero64 = jnp.zeros((HEAD_DIM, HEAD_DIM), F32)

    @pl.when(n == 0)
    def _():
        for p in range(N_PAIRS):
            top = jnp.concatenate([zero64, dn0_ref[2 * p + 1]], axis=1)
            bot = jnp.concatenate([dn0_ref[2 * p], zero64], axis=1)
            z_ref[p] = jnp.concatenate([top, bot], axis=0)
        ext_ref[pad:8, :] = conv0_ref[...]

    ext_ref[8:8 + rows, :] = ha_ref[...]
    for r0 in range(0, rows, LANES):
        for part, dst in enumerate((qn_ref, kn_ref, v_ref)):
            cs = slice(part * HEADS_WIDTH, (part + 1) * HEADS_WIDTH)
            conv = ext_ref[pad + r0:pad + r0 + LANES, cs] * wconv_ref[0:1, cs]
            for j in range(1, DN_CONV):
                conv = conv + ext_ref[pad + j + r0:pad + j + r0 + LANES, cs] * wconv_ref[j:j + 1, cs]
            act = _silu(conv)
            if part < 2:
                inv = lax.rsqrt(_dot((act * act).astype(BF16), hones_ref[...]) + RMS_EPS)
                act = act * (inv * HEAD_DIM ** -0.5 if part == 0 else inv)
            dst[r0:r0 + LANES, :] = act
    ext_ref[pad:8, :] = ext_ref[8 + rows - (DN_CONV - 1):8 + rows, :]

    hg = hg_ref[...]
    neg_rate = jnp.where(lane < N_HEADS, -jnp.exp(alog_ref[...]), 0.0)
    gc = _dot(tril_ref[...], neg_rate * jax.nn.softplus(hg + dtb_ref[...]), HIGHEST)
    gc_ref[...] = gc
    beta_ref[...] = jax.nn.sigmoid(hg)
    for blk in range(rows // LANES):
        gct_ref[blk] = gc[blk * LANES:(blk + 1) * LANES, :].T[0:8, :]

    row_i = lax.broadcasted_iota(jnp.int32, (c, 1), 0)
    col_j = lane % HEAD_DIM
    incl = row_i >= col_j
    strict = row_i > col_j
    pr = lax.broadcasted_iota(jnp.int32, (LANES, LANES), 0) // HEAD_DIM
    pc = lax.broadcasted_iota(jnp.int32, (LANES, LANES), 1) // HEAD_DIM
    anti = pr != pc
    swap = lambda x: jnp.concatenate([x[HEAD_DIM:], x[:HEAD_DIM]], axis=0)
    halves = lambda x: jnp.concatenate([jnp.where(lo, x, jnp.zeros_like(x)), jnp.where(lo, jnp.zeros_like(x), x)],
                                       axis=0)
    problems = [(s, p) for s in range(STAGED_CHUNKS) for p in range(N_PAIRS)]

    def gate_cols(s, p):
        rs = slice(s * c, (s + 1) * c)
        e, o = 2 * p, 2 * p + 1
        gcs = gc_ref[rs, :]
        g_e, g_o = gcs[:, e:e + 1], gcs[:, o:o + 1]
        return g_e, g_o, g_e[c - 1:c, :], g_o[c - 1:c, :]

    for gi, (s, p) in enumerate(problems):
        rs = slice(s * c, (s + 1) * c)
        ls = slice(p * LANES, (p + 1) * LANES)
        e, o = 2 * p, 2 * p + 1
        kp, vp, qp = kn_ref[rs, ls], v_ref[rs, ls], qn_ref[rs, ls]
        ksw = pltpu.roll(kp, HEAD_DIM, axis=1)
        g_e, g_o, _, _ = gate_cols(s, p)
        betas = beta_ref[rs, :]
        b_e, b_o = betas[:, N_HEADS + e:N_HEADS + e + 1], betas[:, N_HEADS + o:N_HEADS + o + 1]
        gt = gct_ref[s // 2]
        gt_sw = pltpu.roll(gt, HEAD_DIM, axis=1)
        if s % 2 == 0:
            g_row = jnp.where(lo, gt[e:e + 1, :], gt_sw[o:o + 1, :])
        else:
            g_row = jnp.where(lo, gt_sw[e:e + 1, :], gt[o:o + 1, :])
        diff = jnp.where(lo, g_e, g_o) - g_row
        decay = jnp.where(incl, jnp.exp(jnp.where(incl, diff, 0.0)), 0.0)
        kb = kp * jnp.where(lo, b_e, b_o)
        st = _dot_nt(jnp.concatenate([kb, qp], axis=0).astype(BF16), halves(kp.astype(BF16)))
        pw_ref[gi] = jnp.where(strict, -(st[0:c] * decay), 0.0).astype(BF16)
        a2_ref[gi] = jnp.where(incl, st[c:2 * c] * decay, 0.0).astype(BF16)
        sol_ref[gi, :, 0:LANES] = jnp.where(lo, vp, ksw * jnp.exp(g_e)) * b_e
        sol_ref[gi, :, LANES:2 * LANES] = jnp.where(lo, ksw * jnp.exp(g_o), vp) * b_o

    zeros_cl = jnp.zeros((c, LANES), BF16)
    n_stage = int(math.log2(c))
    for stage in range(n_stage):
        tail = stage == n_stage - 1
        for gi in range(STAGED_PROBLEMS):
            power = pw_ref[gi]
            sol = sol_ref[gi]
            sb = sol.astype(BF16)
            top = [sb[:, 0:LANES], zeros_cl] + ([] if tail else [jnp.where(lo, power, zeros_cl)])
            bot = [zeros_cl, sb[:, LANES:2 * LANES]] + ([] if tail else [jnp.where(lo, zeros_cl, power)])
            rhs = jnp.concatenate([jnp.concatenate(top, axis=1), jnp.concatenate(bot, axis=1)], axis=0)
            res = _dot(power, rhs)
            sol_ref[gi] = sol + res[:, 0:2 * LANES]
            if not tail:
                pw_ref[gi] = res[:, 2 * LANES:3 * LANES].astype(BF16)

    for gi, (s, p) in enumerate(problems):
        rs = slice(s * c, (s + 1) * c)
        ls = slice(p * LANES, (p + 1) * LANES)
        sol = sol_ref[gi]
        sol_e, sol_o = sol[:, 0:LANES], sol[:, LANES:2 * LANES]
        g_e, g_o, gl_e, gl_o = gate_cols(s, p)
        top = jnp.concatenate([jnp.where(lo, sol_e, 0.0), jnp.where(lo, 0.0, sol_e)], axis=1)
        bot = jnp.concatenate([jnp.where(lo, 0.0, sol_o), jnp.where(lo, sol_o, 0.0)], axis=1)
        y = _dot(a2_ref[gi], jnp.concatenate([top, bot], axis=0).astype(BF16))
        ob_ref[gi] = y[:, 0:LANES]
        qsw = pltpu.roll(qn_ref[rs, ls], HEAD_DIM, axis=1)
        qe_ref[gi] = (qsw * jnp.where(lo, jnp.exp(g_o), jnp.exp(g_e)) - y[:, LANES:2 * LANES]).astype(BF16)
        kt = kn_ref[rs, ls] * jnp.exp(jnp.where(lo, gl_e - g_e, gl_o - g_o))
        kbm = _dot_tn(halves(kt.astype(BF16)), jnp.concatenate([sol_e, sol_o], axis=0).astype(BF16))
        kmat_ref[gi] = jnp.where(anti, kbm, 0.0).astype(BF16)
        bsw_ref[gi] = swap(jnp.where(anti, 0.0, kbm))
        egl_ref[gi] = jnp.exp(jnp.where(lo, gl_e, gl_o))

    for s in range(STAGED_CHUNKS):
        rs = slice(s * c, (s + 1) * c)
        for p in range(N_PAIRS):
            gi = s * N_PAIRS + p
            ls = slice(p * LANES, (p + 1) * LANES)
            zst = z_ref[p]
            zb = zst.astype(BF16)
            out = ob_ref[gi] + _dot(qe_ref[gi], zb)
            z_ref[p] = zst * egl_ref[gi] - swap(_dot(kmat_ref[gi], zb)) + bsw_ref[gi]
            ms = _dot((out * out).astype(BF16), pones_ref[...]) * (1.0 / HEAD_DIM)
            oa_ref[rs, ls] = out * lax.rsqrt(ms + RMS_EPS) * ng_ref[:, ls] * _silu(hz_ref[rs, ls])

    @pl.when(n == pl.num_programs(1) - 1)
    def _():
        for p in range(N_PAIRS):
            dn_ref[2 * p] = z_ref[p][HEAD_DIM:, 0:HEAD_DIM]
            dn_ref[2 * p + 1] = z_ref[p][0:HEAD_DIM, HEAD_DIM:]


def _delta_staged(ha, hz, hg, conv0, dn0, w_conv, a_log, dt_bias, norm_g):
    b, l, _ = ha.shape
    rows, c, g = STAGED_ROWS, CHUNK, STAGED_PROBLEMS
    ri, ci = jnp.arange(rows)[:, None], jnp.arange(rows)[None, :]
    tril = ((ri >= ci) & (ri // c == ci // c)).astype(F32)
    hi, hj = jnp.arange(HEADS_WIDTH)[:, None] // HEAD_DIM, jnp.arange(HEADS_WIDTH)[None, :] // HEAD_DIM
    head_ones = (hi == hj).astype(BF16)
    pair_ones = head_ones[0:LANES, 0:LANES]
    row = lambda w: pl.BlockSpec((None, rows, w), lambda i, n: (i, n, 0))
    whole = lambda shape: pl.BlockSpec(shape, lambda i, n: (0,) * len(shape))
    state = pl.BlockSpec((None, N_HEADS, HEAD_DIM, HEAD_DIM), lambda i, n: (i, 0, 0, 0))
    return pl.pallas_call(
        _delta_staged_kernel,
        grid=(b, l // rows),
        in_specs=[row(QKV_WIDTH), row(HEADS_WIDTH), row(GATE_WIDTH),
                  pl.BlockSpec((None, DN_CONV - 1, QKV_WIDTH), lambda i, n: (i, 0, 0)), state,
                  whole((DN_CONV, QKV_WIDTH)), whole((1, GATE_WIDTH)), whole((1, GATE_WIDTH)),
                  whole((1, HEADS_WIDTH)), whole((rows, rows)), whole((HEADS_WIDTH, HEADS_WIDTH)),
                  whole((LANES, LANES))],
        out_specs=[row(HEADS_WIDTH), state],
        out_shape=[jax.ShapeDtypeStruct((b, l, HEADS_WIDTH), F32),
                   jax.ShapeDtypeStruct((b, N_HEADS, HEAD_DIM, HEAD_DIM), F32)],
        scratch_shapes=[pltpu.VMEM((N_PAIRS, LANES, LANES), F32),
                        pltpu.VMEM((8 + rows, QKV_WIDTH), F32),
                        pltpu.VMEM((rows, HEADS_WIDTH), F32),
                        pltpu.VMEM((rows, HEADS_WIDTH), F32),
                        pltpu.VMEM((rows, HEADS_WIDTH), F32),
                        pltpu.VMEM((rows, GATE_WIDTH), F32),
                        pltpu.VMEM((rows, GATE_WIDTH), F32),
                        pltpu.VMEM((rows // LANES, 8, LANES), F32),
                        pltpu.VMEM((g, c, LANES), BF16),
                        pltpu.VMEM((g, c, 2 * LANES), F32),
                        pltpu.VMEM((g, c, LANES), BF16),
                        pltpu.VMEM((g, c, LANES), F32),
                        pltpu.VMEM((g, c, LANES), BF16),
                        pltpu.VMEM((g, LANES, LANES), BF16),
                        pltpu.VMEM((g, LANES, LANES), F32),
                        pltpu.VMEM((g, 1, LANES), F32)],
        compiler_params=_params("arbitrary", "arbitrary"),
        name="delta_rule_staged",
    )(ha, hz, hg, conv0, dn0, w_conv, a_log, dt_bias, norm_g, tril, head_ones, pair_ones)


ATTN_WINDOW = BAND_ROWS + PAIR_ROWS


def _attn_pairs_kernel(q_ref, kp_ref, ks_ref, vp_ref, vs_ref, bias_ref, o_ref, kwin_ref, vwin_ref):
    i = pl.program_id(1)
    w, tq = BAND_ROWS, ROW_TILE
    kwin_ref[0:w, :] = kp_ref[...]
    kwin_ref[w:w + tq, :] = ks_ref[...]
    vwin_ref[0:w, :] = vp_ref[...]
    vwin_ref[w:w + tq, :] = vs_ref[...]
    lo = lax.broadcasted_iota(jnp.int32, (1, LANES), 1) < HEAD_DIM
    key_idx = lax.broadcasted_iota(jnp.int32, (1, ATTN_WINDOW), 1)

    def chunk_pair(m, carry):
        r0 = pl.multiple_of(m * PAIR_ROWS, PAIR_ROWS)
        before_start = (i * tq + r0 + key_idx - w) < 0
        for p in range(N_PAIRS):
            ls = slice(p * LANES, (p + 1) * LANES)
            q = q_ref[pl.ds(r0, PAIR_ROWS), ls] * HEAD_DIM ** -0.5
            kk = kwin_ref[pl.ds(r0, ATTN_WINDOW), ls]
            vv = vwin_ref[pl.ds(r0, ATTN_WINDOW), ls]
            zero = jnp.zeros_like(q)
            q2 = jnp.concatenate([jnp.where(lo, q, zero), jnp.where(lo, zero, q)], axis=0)
            s = _dot_nt(q2, kk) + bias_ref[p]
            s = jnp.where(before_start, -1e30, s)
            pexp = jnp.exp(s - jnp.max(s, axis=-1, keepdims=True))
            denom = jnp.sum(pexp, axis=-1, keepdims=True)
            pv = _dot(pexp.astype(BF16), vv) / denom
            o_ref[pl.ds(r0, PAIR_ROWS), ls] = jnp.where(lo, pv[0:PAIR_ROWS], pv[PAIR_ROWS:2 * PAIR_ROWS])
        return carry

    lax.fori_loop(0, tq // PAIR_ROWS, chunk_pair, 0)


def _pair_bias(rel_table):
    a = jnp.arange(PAIR_ROWS)[:, None]
    idx = jnp.arange(ATTN_WINDOW)[None, :]
    first = (a // CHUNK) * CHUNK
    in_band = (idx >= first) & (idx < first + BAND_ROWS + CHUNK)
    bias = jnp.where(in_band[None], _rel_bias(rel_table, PAIR_ROWS), -1e30)
    return bias.reshape(N_PAIRS, 2 * PAIR_ROWS, ATTN_WINDOW)


def _attention_pairs(qkv, bias):
    b, l, _ = qkv.shape
    tq = ROW_TILE
    assert tq == BAND_ROWS and l % tq == 0
    own = lambda j: pl.BlockSpec((None, tq, HEADS_WIDTH), lambda bi, i: (bi, i, j))
    prev = lambda j: pl.BlockSpec((None, BAND_ROWS, HEADS_WIDTH), lambda bi, i: (bi, jnp.maximum(i - 1, 0), j))
    return pl.pallas_call(
        _attn_pairs_kernel,
        grid=(b, l // tq),
        in_specs=[own(0), prev(1), own(1), prev(2), own(2),
                  pl.BlockSpec((N_PAIRS, 2 * PAIR_ROWS, ATTN_WINDOW), lambda bi, i: (0, 0, 0))],
        out_specs=own(0),
        out_shape=jax.ShapeDtypeStruct((b, l, HEADS_WIDTH), F32),
        scratch_shapes=[pltpu.VMEM((BAND_ROWS + tq, HEADS_WIDTH), BF16),
                        pltpu.VMEM((BAND_ROWS + tq, HEADS_WIDTH), BF16)],
        compiler_params=_params("arbitrary", "arbitrary"),
        name="band_attention_pairs",
    )(qkv, qkv, qkv, qkv, qkv, bias)


def _mix_kernel(tile, start, full_f32, oa_ref, ob_ref, hc_ref, x_ref, pool0_ref, pw_ref, ps_ref, wout_ref, g_ref, b_ref,
                x1_ref, ext_ref):
    i = pl.program_id(1)
    prec = HIGHEST if full_f32 else None
    wdt = wout_ref.dtype
    hist0 = 16 - POOL_HIST

    @pl.when(i == 0)
    def _():
        ext_ref[0:hist0, :] = jnp.zeros((hist0, POOL_WIDTH), F32)
        ext_ref[hist0:16, :] = pool0_ref[...]

    u = hc_ref[...]
    ext_ref[16:16 + tile, :] = u
    window = 2 << (lax.broadcasted_iota(jnp.int32, (1, POOL_WIDTH), 1) // POOL_GROUP_DIM)
    level = ext_ref[...]
    sums = {}
    for wdw in POOL_WINDOWS:
        level = level + pltpu.roll(level, wdw // 2, axis=0)
        sums[wdw] = level[16:16 + tile, :]
    wsum = sums[POOL_WINDOWS[-1]]
    for wdw in reversed(POOL_WINDOWS[:-1]):
        wsum = jnp.where(window == wdw, sums[wdw], wsum)
    ext_ref[hist0:16, :] = u[tile - POOL_HIST:tile, :]
    pos1 = start + i * tile + 1 + lax.broadcasted_iota(jnp.int32, (tile, 1), 0)
    cnt = jnp.minimum(pos1, window).astype(F32)
    res = wsum / cnt - u
    oc = _dot(res, pw_ref[...], prec) * ps_ref[...]

    mix = _dot(oa_ref[...].astype(wdt), wout_ref[0:HEADS_WIDTH, :], prec)
    mix = mix + _dot(ob_ref[...].astype(wdt), wout_ref[HEADS_WIDTH:2 * HEADS_WIDTH, :], prec)
    mix = mix + _dot(oc.astype(wdt), wout_ref[2 * HEADS_WIDTH:D_MODEL, :], prec)
    x1_ref[...] = _layer_norm(DEEPNORM_ALPHA * x_ref[...] + mix, g_ref[...], b_ref[...])


def _mix(oa, ob, hc, x, pool0, pool_wbd, pool_scale, w_out, ln_g, ln_b, start, full_f32):
    b, l, _ = x.shape
    tile = min(ROW_TILE, l)
    row = lambda w: pl.BlockSpec((None, tile, w), lambda bi, i: (bi, i, 0))
    whole = lambda shape: pl.BlockSpec(shape, lambda bi, i: (0,) * len(shape))
    return pl.pallas_call(
        functools.partial(_mix_kernel, tile, start, full_f32),
        grid=(b, l // tile),
        in_specs=[row(HEADS_WIDTH), row(HEADS_WIDTH), row(POOL_WIDTH), row(D_MODEL),
                  pl.BlockSpec((None, POOL_HIST, POOL_WIDTH), lambda bi, i: (bi, 0, 0)),
                  whole((POOL_WIDTH, POOL_WIDTH)), whole((1, POOL_WIDTH)), whole((D_MODEL, D_MODEL)),
                  whole((1, D_MODEL)), whole((1, D_MODEL))],
        out_specs=row(D_MODEL),
        out_shape=jax.ShapeDtypeStruct((b, l, D_MODEL), F32),
        scratch_shapes=[pltpu.VMEM((16 + tile, POOL_WIDTH), F32)],
        compiler_params=_params("arbitrary", "arbitrary"),
        name="pool_outproj_ln1",
    )(oa, ob, hc, x, pool0, pool_wbd, pool_scale, w_out, ln_g, ln_b)


def _top2_of4(a, b, c, d):
    hi1, lo1 = jnp.maximum(a, b), jnp.minimum(a, b)
    hi2, lo2 = jnp.maximum(c, d), jnp.minimum(c, d)
    return jnp.maximum(hi1, hi2), jnp.maximum(jnp.minimum(hi1, hi2), jnp.maximum(lo1, lo2))


def _route(logits_t, bias_t):
    aff = jax.nn.sigmoid(logits_t[0:N_EXPERTS, :])
    sel = aff + bias_t[0:N_EXPERTS, :]
    t = sel.shape[1]
    scores = []
    for gi in range(N_EXPERT_GROUPS):
        r = [sel[4 * gi + m:4 * gi + m + 1, :] for m in range(EXPERTS_PER_GROUP)]
        top1, top2 = _top2_of4(*r)
        scores.append(top1 + top2)
    best = scores[0]
    best_g = jnp.zeros((1, t), jnp.int32)
    for gi in range(1, N_EXPERT_GROUPS):
        better = scores[gi] > best
        best = jnp.where(better, scores[gi], best)
        best_g = jnp.where(better, gi, best_g)
    e_idx = lax.broadcasted_iota(jnp.int32, (N_EXPERTS, t), 0)
    masked = jnp.where(e_idx // EXPERTS_PER_GROUP == best_g, sel, -jnp.inf)
    m1 = jnp.max(masked, axis=0, keepdims=True)
    i1 = jnp.min(jnp.where(masked == m1, e_idx, N_EXPERTS), axis=0, keepdims=True)
    rest = jnp.where(e_idx == i1, -jnp.inf, masked)
    m2 = jnp.max(rest, axis=0, keepdims=True)
    i2 = jnp.min(jnp.where(rest == m2, e_idx, N_EXPERTS), axis=0, keepdims=True)
    w1 = jnp.sum(jnp.where(e_idx == i1, aff, 0.0), axis=0, keepdims=True)
    w2 = jnp.sum(jnp.where(e_idx == i2, aff, 0.0), axis=0, keepdims=True)
    tot = w1 + w2
    gates = jnp.where(e_idx == i1, w1 / tot, 0.0) + jnp.where(e_idx == i2, w2 / tot, 0.0)
    return jnp.concatenate([gates, jnp.zeros((LANES - N_EXPERTS, t), F32)], axis=0), best_g


def _expert(xb, w_gu, w_d, gates, e, prec):
    gu = _dot(xb, w_gu, prec)
    hid = _silu(gu[:, 0:D_EXPERT]) * gu[:, D_EXPERT:2 * D_EXPERT]
    lane = lax.broadcasted_iota(jnp.int32, (1, LANES), 1)
    ge = jnp.sum(jnp.where(lane == e, gates, 0.0), axis=-1, keepdims=True)
    return _dot(hid.astype(w_d.dtype), w_d, prec) * ge


def _ffn_kernel(x_ref, wr_ref, rb_ref, wgu_ref, wd_ref, g_ref, b_ref, y_ref):
    x = x_ref[...]
    xb = x.astype(BF16)
    logits = _dot(xb, wr_ref[...])
    gates = _route(logits.T, rb_ref[...])[0].T
    acc = jnp.zeros(x.shape, F32)
    for e in range(N_EXPERTS):
        acc = acc + _expert(xb, wgu_ref[e], wd_ref[e], gates, e, None)
    y_ref[...] = _layer_norm(DEEPNORM_ALPHA * x + acc, g_ref[...], b_ref[...])


def _ffn_f32_kernel(x_ref, wr_ref, rb_ref, wgu_ref, wd_ref, g_ref, b_ref, y_ref, gates_ref, acc_ref):
    e = pl.program_id(1)
    x = x_ref[...]

    @pl.when(e == 0)
    def _():
        logits = _dot(x, wr_ref[...], HIGHEST)
        gates_ref[...] = _route(logits.T, rb_ref[...])[0].T
        acc_ref[...] = jnp.zeros(acc_ref.shape, F32)

    acc_ref[...] += _expert(x, wgu_ref[...], wd_ref[...], gates_ref[...], e, HIGHEST)

    @pl.when(e == N_EXPERTS - 1)
    def _():
        y_ref[...] = _layer_norm(DEEPNORM_ALPHA * x + acc_ref[...], g_ref[...], b_ref[...])


def _ffn(x2d, w_router, router_bias, w_gu, w_down, ln_g, ln_b, full_f32):
    t = x2d.shape[0]
    tm = min(ROW_TILE, t)
    out_shape = jax.ShapeDtypeStruct((t, D_MODEL), F32)
    if full_f32:
        whole = lambda shape: pl.BlockSpec(shape, lambda i, e: (0,) * len(shape))
        return pl.pallas_call(
            _ffn_f32_kernel,
            grid=(t // tm, N_EXPERTS),
            in_specs=[pl.BlockSpec((tm, D_MODEL), lambda i, e: (i, 0)),
                      whole((D_MODEL, LANES)), whole((LANES, 1)),
                      pl.BlockSpec((None, D_MODEL, 2 * D_EXPERT), lambda i, e: (e, 0, 0)),
                      pl.BlockSpec((None, D_EXPERT, D_MODEL), lambda i, e: (e, 0, 0)),
                      whole((1, D_MODEL)), whole((1, D_MODEL))],
            out_specs=pl.BlockSpec((tm, D_MODEL), lambda i, e: (i, 0)),
            out_shape=out_shape,
            scratch_shapes=[pltpu.VMEM((tm, LANES), F32), pltpu.VMEM((tm, D_MODEL), F32)],
            compiler_params=_params("arbitrary", "arbitrary"),
            name="routed_ffn_ln2_f32",
        )(x2d, w_router, router_bias, w_gu, w_down, ln_g, ln_b)
    whole = lambda shape: pl.BlockSpec(shape, lambda i: (0,) * len(shape), pipeline_mode=pl.Buffered(1))
    return pl.pallas_call(
        _ffn_kernel,
        grid=(t // tm,),
        in_specs=[pl.BlockSpec((tm, D_MODEL), lambda i: (i, 0)),
                  whole((D_MODEL, LANES)), whole((LANES, 1)),
                  whole((N_EXPERTS, D_MODEL, 2 * D_EXPERT)), whole((N_EXPERTS, D_EXPERT, D_MODEL)),
                  whole((1, D_MODEL)), whole((1, D_MODEL))],
        out_specs=pl.BlockSpec((tm, D_MODEL), lambda i: (i, 0)),
        out_shape=out_shape,
        compiler_params=_params("arbitrary"),
        name="routed_ffn_ln2",
    )(x2d, w_router, router_bias, w_gu, w_down, ln_g, ln_b)


def _prep_layer(l, wdt, w_in, w_conv, a_log, dt_bias, dn_norm_g, rel_table, pool_w, pool_scale, w_out,
                ln1_g, ln1_b, w_gate, w_up, w_down, ln2_g, ln2_b):
    wi = w_in[l]
    gates = jnp.zeros((D_MODEL, GATE_WIDTH), F32).at[:, 0:2 * N_HEADS].set(wi[:, OFF_AA:OFF_BQ])
    w_perm = jnp.concatenate([wi[:, OFF_AQ:OFF_AZ], wi[:, OFF_AZ:OFF_AA], wi[:, OFF_BQ:OFF_CU],
                              wi[:, OFF_CU:IN_WIDTH], gates], axis=1).astype(wdt)
    pad_heads = lambda vec: jnp.zeros((1, GATE_WIDTH), F32).at[0, 0:N_HEADS].set(vec)
    pool_wbd = jnp.zeros((POOL_WIDTH, POOL_WIDTH), F32)
    for gi in range(len(POOL_WINDOWS)):
        s = slice(gi * POOL_GROUP_DIM, (gi + 1) * POOL_GROUP_DIM)
        pool_wbd = pool_wbd.at[s, s].set(pool_w[l, gi])
    return dict(
        w_perm=w_perm, w_conv=w_conv[l], a_log=pad_heads(a_log[l]), dt_bias=pad_heads(dt_bias[l]),
        norm_g=jnp.tile(dn_norm_g[l], N_HEADS)[None, :], rel_table=rel_table[l],
        pool_wbd=pool_wbd, pool_scale=pool_scale[l][None, :], w_out=w_out[l].astype(wdt),
        ln1_g=ln1_g[l][None, :], ln1_b=ln1_b[l][None, :],
        w_gu=jnp.concatenate([w_gate[l], w_up[l]], axis=-1).astype(wdt), w_down=w_down[l].astype(wdt),
        ln2_g=ln2_g[l][None, :], ln2_b=ln2_b[l][None, :])


def _rel_bias(rel_table, c):
    w = BAND_ROWS + c
    period = c + w
    m = jnp.arange(period)
    delta = jnp.where(m < w, -m, period - m)
    vec = rel_table[:, jnp.clip(delta + BAND_ROWS, -REL_CLIP, REL_CLIP) + REL_CLIP].astype(F32)
    rolled = jnp.tile(vec, (1, c))[:, :c * (period - 1)].reshape(rel_table.shape[0], c, period - 1)
    return rolled[:, :, :w]


def _last_rows(t, n):
    if t.shape[1] < n:
        t = jnp.concatenate([jnp.zeros((t.shape[0], n - t.shape[1]) + t.shape[2:], t.dtype), t], axis=1)
    return t[:, t.shape[1] - n:]


def _trunk(x, start, dn_state, conv_state, k_hist, v_hist, pool_state, band_rows, layers, w_router, router_bias,
           full_f32):
    b, l, _ = x.shape
    assert l >= POOL_HIST
    c = min(CHUNK, l)
    new_dn, new_conv, new_k, new_v, new_pool = [], [], [], [], []
    for li, p in enumerate(layers):
        ha, hz, hb, hc, hg = _inproj(x.reshape(b * l, D_MODEL), p["w_perm"], full_f32)
        ha = ha.reshape(b, l, QKV_WIDTH)
        hz = hz.reshape(b, l, HEADS_WIDTH)
        hb = hb.reshape(b, l, QKV_WIDTH)
        hc = hc.reshape(b, l, POOL_WIDTH)
        hg = hg.reshape(b, l, GATE_WIDTH)
        oa, dn_new = _delta(ha, hz, hg, conv_state[li], dn_state[li], p["w_conv"], p["a_log"], p["dt_bias"],
                            p["norm_g"], full_f32)
        qb, kb, vb = (hb[..., j * HEADS_WIDTH:(j + 1) * HEADS_WIDTH] for j in range(3))
        bias = _rel_bias(p["rel_table"], c)
        if k_hist is None:
            ob = _attention(qb, kb, vb, None, None, bias, full_f32)
            k_new, v_new = _last_rows(kb, band_rows), _last_rows(vb, band_rows)
        else:
            kh = k_hist[li].reshape(b, -1, HEADS_WIDTH)
            vh = v_hist[li].reshape(b, -1, HEADS_WIDTH)
            ob = _attention(qb, kb, vb, kh, vh, bias, full_f32)
            k_new = _last_rows(jnp.concatenate([kh, kb], axis=1), band_rows)
            v_new = _last_rows(jnp.concatenate([vh, vb], axis=1), band_rows)
        x1 = _mix(oa, ob, hc, x, pool_state[li], p["pool_wbd"], p["pool_scale"], p["w_out"], p["ln1_g"],
                  p["ln1_b"], start, full_f32)
        x = _ffn(x1.reshape(b * l, D_MODEL), w_router.astype(p["w_out"].dtype), router_bias, p["w_gu"],
                 p["w_down"], p["ln2_g"], p["ln2_b"], full_f32).reshape(b, l, D_MODEL)
        new_dn.append(dn_new)
        new_conv.append(ha[:, l - (DN_CONV - 1):])
        new_k.append(k_new.reshape(b, band_rows, N_HEADS, HEAD_DIM))
        new_v.append(v_new.reshape(b, band_rows, N_HEADS, HEAD_DIM))
        new_pool.append(hc[:, l - POOL_HIST:])
    return (x, jnp.stack(new_dn), jnp.stack(new_conv), jnp.stack(new_k), jnp.stack(new_v), jnp.stack(new_pool))


def _trunk_seq(x, dn_state, conv_state, pool_state, band_rows, layers, w_router, router_bias):
    b, l, _ = x.shape
    assert band_rows == BAND_ROWS and l >= band_rows
    new_dn, new_conv, new_k, new_v, new_pool = [], [], [], [], []
    for li, p in enumerate(layers):
        ha, hz, hb, hc, hg, kv_last = _inproj_seq(x, p["w_perm"])
        oa, dn_new = _delta_staged(ha, hz, hg, conv_state[li], dn_state[li], p["w_conv"], p["a_log"], p["dt_bias"],
                                   p["norm_g"])
        ob = _attention_pairs(hb, _pair_bias(p["rel_table"]))
        x1 = _mix(oa, ob, hc, x, pool_state[li], p["pool_wbd"], p["pool_scale"], p["w_out"], p["ln1_g"],
                  p["ln1_b"], 0, False)
        x = _ffn(x1.reshape(b * l, D_MODEL), w_router.astype(BF16), router_bias, p["w_gu"], p["w_down"],
                 p["ln2_g"], p["ln2_b"], False).reshape(b, l, D_MODEL)
        new_dn.append(dn_new)
        new_conv.append(ha[:, l - (DN_CONV - 1):])
        new_k.append(kv_last[..., 0:HEADS_WIDTH].reshape(b, band_rows, N_HEADS, HEAD_DIM))
        new_v.append(kv_last[..., HEADS_WIDTH:].reshape(b, band_rows, N_HEADS, HEAD_DIM))
        new_pool.append(hc[:, l - POOL_HIST:])
    return (x, jnp.stack(new_dn), jnp.stack(new_conv), jnp.stack(new_k), jnp.stack(new_v), jnp.stack(new_pool))


@jax.jit
def kernel(x_prompt, x_sample, state_dn, state_conv, cache_k, cache_v, state_pool, w_in, w_conv, a_log, dt_bias,
           dn_norm_g, rel_table, pool_w, pool_scale, w_out, ln1_g, ln1_b, w_router, router_bias, w_gate, w_up,
           w_down, ln2_g, ln2_b):
    weights = (w_in, w_conv, a_log, dt_bias, dn_norm_g, rel_table, pool_w, pool_scale, w_out,
               ln1_g, ln1_b, w_gate, w_up, w_down, ln2_g, ln2_b)
    layers_bf16 = [_prep_layer(l, BF16, *weights) for l in range(DEPTH)]
    layers_f32 = [_prep_layer(l, F32, *weights) for l in range(DEPTH)]
    w_router_p = jnp.zeros((D_MODEL, LANES), F32).at[:, 0:N_EXPERTS].set(w_router)
    router_bias_p = jnp.zeros((LANES, 1), F32).at[0:N_EXPERTS, 0].set(router_bias)
    band_rows = cache_k.shape[2]
    bp = x_prompt.shape[0]
    zero_dn = jnp.zeros((DEPTH, bp, N_HEADS, HEAD_DIM, HEAD_DIM), state_dn.dtype)
    zero_conv = jnp.zeros((DEPTH, bp, DN_CONV - 1, QKV_WIDTH), x_prompt.dtype)
    zero_pool = jnp.zeros((DEPTH, bp, POOL_HIST, POOL_WIDTH), x_prompt.dtype)
    prompt = _trunk_seq(x_prompt, zero_dn, zero_conv, zero_pool, band_rows, layers_bf16, w_router_p, router_bias_p)
    sample = _trunk(x_sample, PAST_LEN, state_dn, state_conv, cache_k, cache_v, state_pool, band_rows, layers_f32,
                    w_router_p, router_bias_p, True)
    return (prompt[0], sample[0]) + prompt[1:] + sample[1:]
```

```python
import functools
import math

import jax
import jax.numpy as jnp
from jax import lax
from jax.experimental import pallas as pl
from jax.experimental.pallas import tpu as pltpu

F32 = jnp.float32
BF16 = jnp.bfloat16
HIGHEST = lax.Precision.HIGHEST

D_MODEL = 1024
HEAD_DIM = 64
N_HEADS = 6
HEADS_WIDTH = N_HEADS * HEAD_DIM
QKV_WIDTH = 3 * HEADS_WIDTH
DN_CONV = 4
CHUNK = 64
BAND_ROWS = 512
REL_CLIP = 128
POOL_WINDOWS = (2, 4, 8, 16)
POOL_GROUP_DIM = 64
POOL_WIDTH = 256
POOL_HIST = 15
N_EXPERTS = 16
N_EXPERT_GROUPS = 4
EXPERTS_PER_GROUP = 4
D_EXPERT = 256
DEPTH = 2
DEEPNORM_ALPHA = (2 * DEPTH) ** 0.25
LN_EPS = 1e-5
RMS_EPS = 1e-6
PAST_LEN = 1024

OFF_AQ = 0
OFF_AZ = 3 * HEADS_WIDTH
OFF_AA = OFF_AZ + HEADS_WIDTH
OFF_AB = OFF_AA + N_HEADS
OFF_BQ = OFF_AB + N_HEADS
OFF_CU = OFF_BQ + 3 * HEADS_WIDTH
IN_WIDTH = OFF_CU + POOL_WIDTH

LANES = 128
GATE_WIDTH = LANES
SEG_A = (0, QKV_WIDTH)
SEG_Z = (SEG_A[1], SEG_A[1] + HEADS_WIDTH)
SEG_B = (SEG_Z[1], SEG_Z[1] + QKV_WIDTH)
SEG_C = (SEG_B[1], SEG_B[1] + POOL_WIDTH)
SEG_G = (SEG_C[1], SEG_C[1] + GATE_WIDTH)
PERM_WIDTH = SEG_G[1]

ROW_TILE = 512
VMEM_LIMIT = 56 * 1024 * 1024


def _params(*sem):
    return pltpu.CompilerParams(dimension_semantics=sem, vmem_limit_bytes=VMEM_LIMIT)


def _dot(a, b, precision=None):
    return jnp.dot(a, b, preferred_element_type=F32, precision=precision)


def _dot_nt(a, b, precision=None):
    return lax.dot_general(a, b, (((1,), (1,)), ((), ())), preferred_element_type=F32, precision=precision)


def _dot_tn(a, b, precision=None):
    return lax.dot_general(a, b, (((0,), (0,)), ((), ())), preferred_element_type=F32, precision=precision)


def _silu(x):
    return x * jax.nn.sigmoid(x)


def _layer_norm(x, g, b):
    mu = jnp.mean(x, axis=-1, keepdims=True)
    xc = x - mu
    var = jnp.mean(xc * xc, axis=-1, keepdims=True)
    return xc * lax.rsqrt(var + LN_EPS) * g + b


def _inproj_kernel(full_f32, x_ref, w_ref, ha_ref, hz_ref, hb_ref, hc_ref, hg_ref):
    prec = HIGHEST if full_f32 else None
    xb = x_ref[...].astype(w_ref.dtype)
    for (lo, hi), out_ref in ((SEG_A, ha_ref), (SEG_Z, hz_ref), (SEG_B, hb_ref), (SEG_C, hc_ref), (SEG_G, hg_ref)):
        step = 384 if (hi - lo) % 384 == 0 else hi - lo
        for c0 in range(0, hi - lo, step):
            out_ref[:, c0:c0 + step] = _dot(xb, w_ref[:, lo + c0:lo + c0 + step], prec)


def _inproj(x2d, w_perm, full_f32):
    t = x2d.shape[0]
    tm = min(ROW_TILE, t)
    widths = [s[1] - s[0] for s in (SEG_A, SEG_Z, SEG_B, SEG_C, SEG_G)]
    return pl.pallas_call(
        functools.partial(_inproj_kernel, full_f32),
        grid=(t // tm,),
        in_specs=[pl.BlockSpec((tm, D_MODEL), lambda i: (i, 0)),
                  pl.BlockSpec((D_MODEL, PERM_WIDTH), lambda i: (0, 0))],
        out_specs=[pl.BlockSpec((tm, w), lambda i: (i, 0)) for w in widths],
        out_shape=[jax.ShapeDtypeStruct((t, w), F32) for w in widths],
        compiler_params=_params("arbitrary"),
        name="inproj",
    )(x2d, w_perm)


INPROJ_COLS = 512


def _inproj_seq_kernel(x_ref, w_ref, conv0_ref, wconv_ref, hones_ref,
                       qn_ref, kn_ref, v_ref, hz_ref, hb_ref, hc_ref, hg_ref, kv_ref, cs_ref, ext_ref):
    i = pl.program_id(1)
    tm = x_ref.shape[0]
    pad = 8 - (DN_CONV - 1)

    @pl.when(i == 0)
    def _():
        ext_ref[pad:8, :] = conv0_ref[...]

    xb = x_ref[...].astype(BF16)
    segments = ((SEG_Z, hz_ref, 0), (SEG_B, hb_ref, 0), (SEG_C, hc_ref, 0), (SEG_G, hg_ref, 0), (SEG_A, ext_ref, 8))
    kv0 = SEG_B[0] + HEADS_WIDTH
    for c0 in range(0, PERM_WIDTH, INPROJ_COLS):
        h = _dot(xb, w_ref[:, c0:c0 + INPROJ_COLS])
        for (lo, hi), out_ref, row0 in segments:
            a, b = max(lo, c0), min(hi, c0 + INPROJ_COLS)
            if a < b:
                out_ref[row0:row0 + tm, a - lo:b - lo] = h[:, a - c0:b - c0].astype(out_ref.dtype)
        a, b = max(kv0, c0), min(SEG_B[1], c0 + INPROJ_COLS)
        if a < b:
            kv_ref[:, a - kv0:b - kv0] = h[:, a - c0:b - c0]

    for r0 in range(0, tm, LANES):
        for part, dst in enumerate((qn_ref, kn_ref, v_ref)):
            cs = slice(part * HEADS_WIDTH, (part + 1) * HEADS_WIDTH)
            conv = ext_ref[pad + r0:pad + r0 + LANES, cs] * wconv_ref[0:1, cs]
            for j in range(1, DN_CONV):
                conv = conv + ext_ref[pad + j + r0:pad + j + r0 + LANES, cs] * wconv_ref[j:j + 1, cs]
            act = _silu(conv)
            if part < 2:
                inv = lax.rsqrt(_dot((act * act).astype(BF16), hones_ref[...]) + RMS_EPS)
                act = act * (inv * HEAD_DIM ** -0.5 if part == 0 else inv)
            dst[r0:r0 + LANES, :] = act
    last = ext_ref[8 + tm - (DN_CONV - 1):8 + tm, :]
    cs_ref[...] = last
    ext_ref[pad:8, :] = last


def _inproj_seq(x, w_perm, conv0, w_conv):
    b, l, _ = x.shape
    tm = ROW_TILE
    assert tm == BAND_ROWS and l % tm == 0
    widths = [HEADS_WIDTH] * 4 + [s[1] - s[0] for s in (SEG_B, SEG_C, SEG_G)]
    dtypes = [F32, F32, F32, F32, BF16, F32, F32]
    row = lambda w: pl.BlockSpec((None, tm, w), lambda bi, i: (bi, i, 0))
    whole = lambda shape: pl.BlockSpec(shape, lambda bi, i: (0,) * len(shape))
    conv_rows = pl.BlockSpec((None, DN_CONV - 1, QKV_WIDTH), lambda bi, i: (bi, 0, 0))
    return pl.pallas_call(
        _inproj_seq_kernel,
        grid=(b, l // tm),
        in_specs=[row(D_MODEL), whole((D_MODEL, PERM_WIDTH)), conv_rows, whole((DN_CONV, QKV_WIDTH)),
                  whole((HEADS_WIDTH, HEADS_WIDTH))],
        out_specs=[row(w) for w in widths]
        + [pl.BlockSpec((None, tm, 2 * HEADS_WIDTH), lambda bi, i: (bi, 0, 0)), conv_rows],
        out_shape=[jax.ShapeDtypeStruct((b, l, w), d) for w, d in zip(widths, dtypes)]
        + [jax.ShapeDtypeStruct((b, tm, 2 * HEADS_WIDTH), F32),
           jax.ShapeDtypeStruct((b, DN_CONV - 1, QKV_WIDTH), F32)],
        scratch_shapes=[pltpu.VMEM((8 + tm, QKV_WIDTH), F32)],
        compiler_params=_params("arbitrary", "arbitrary"),
        name="inproj_seq",
    )(x, w_perm, conv0, w_conv, _head_ones())


def _delta_kernel(c, full_f32, ha_ref, hz_ref, hg_ref, conv0_ref, dn0_ref, wconv_ref, alog_ref, dtb_ref, ng_ref,
                  oa_ref, dn_ref, s_ref, ext_ref):
    n = pl.program_id(1)
    prec = HIGHEST if full_f32 else None
    pad = 8 - (DN_CONV - 1)

    @pl.when(n == 0)
    def _():
        s_ref[...] = dn0_ref[...]
        ext_ref[pad:8, :] = conv0_ref[...]

    u = ha_ref[...]
    ext_ref[8:8 + c, :] = u
    conv = ext_ref[pad:pad + c, :] * wconv_ref[0:1, :]
    for j in range(1, DN_CONV):
        conv = conv + ext_ref[pad + j:pad + j + c, :] * wconv_ref[j:j + 1, :]
    ext_ref[pad:8, :] = u[c - (DN_CONV - 1):c, :]
    qkv = _silu(conv)
    q = qkv[:, 0:HEADS_WIDTH]
    k = qkv[:, HEADS_WIDTH:2 * HEADS_WIDTH]
    v = qkv[:, 2 * HEADS_WIDTH:QKV_WIDTH]

    r = lax.broadcasted_iota(jnp.int32, (HEADS_WIDTH, HEADS_WIDTH), 0) // HEAD_DIM
    cc = lax.broadcasted_iota(jnp.int32, (HEADS_WIDTH, HEADS_WIDTH), 1) // HEAD_DIM
    head_ones = (r == cc).astype(F32)
    qn = q * lax.rsqrt(_dot(q * q, head_ones, HIGHEST) + RMS_EPS) * HEAD_DIM ** -0.5
    kn = k * lax.rsqrt(_dot(k * k, head_ones, HIGHEST) + RMS_EPS)

    hg = hg_ref[...]
    lane = lax.broadcasted_iota(jnp.int32, (1, GATE_WIDTH), 1)
    neg_rate = jnp.where(lane < N_HEADS, -jnp.exp(alog_ref[...]), 0.0)
    g = neg_rate * jax.nn.softplus(hg + dtb_ref[...])
    beta = jax.nn.sigmoid(hg)
    ri = lax.broadcasted_iota(jnp.int32, (c, c), 0)
    ci = lax.broadcasted_iota(jnp.int32, (c, c), 1)
    incl = ri >= ci
    strict = ri > ci
    gc = _dot(incl.astype(F32), g, HIGHEST)
    gc_t = _dot_tn(g, (ri <= ci).astype(F32), HIGHEST)

    z = hz_ref[...]
    for h in range(N_HEADS):
        sl = slice(h * HEAD_DIM, (h + 1) * HEAD_DIM)
        qh, kh, vh = qn[:, sl], kn[:, sl], v[:, sl]
        gch = gc[:, h:h + 1]
        diff = gch - gc_t[h:h + 1, :]
        decay = jnp.where(incl, jnp.exp(jnp.where(incl, diff, 0.0)), 0.0)
        bh = beta[:, N_HEADS + h:N_HEADS + h + 1]
        kbeta = kh * bh
        eg = jnp.exp(gch)
        neg_m = jnp.where(strict, -(_dot_nt(kbeta, kh, prec) * decay), 0.0)
        sol = jnp.concatenate([vh * bh, kbeta * eg], axis=-1)
        power = neg_m
        sol = sol + _dot(power, sol, prec)
        for _ in range(int(math.log2(c)) - 1):
            power = _dot(power, power, prec)
            sol = sol + _dot(power, sol, prec)
        u_h, w_h = sol[:, 0:HEAD_DIM], sol[:, HEAD_DIM:2 * HEAD_DIM]
        a_intra = jnp.where(incl, _dot_nt(qh, kh, prec) * decay, 0.0)
        g_last = gc[c - 1:c, h:h + 1]
        k_tail = kh * jnp.exp(g_last - gch)
        s = s_ref[h]
        v_new = u_h - _dot(w_h, s, prec)
        o = _dot(qh * eg, s, prec) + _dot(a_intra, v_new, prec)
        s_ref[h] = s * jnp.exp(g_last) + _dot_tn(k_tail, v_new, prec)
        o = o * lax.rsqrt(jnp.mean(o * o, axis=-1, keepdims=True) + RMS_EPS)
        oa_ref[:, sl] = o * ng_ref[:, sl] * _silu(z[:, sl])

    @pl.when(n == pl.num_programs(1) - 1)
    def _():
        dn_ref[...] = s_ref[...]


def _delta(ha, hz, hg, conv0, dn0, w_conv, a_log, dt_bias, norm_g, full_f32):
    b, l, _ = ha.shape
    c = min(CHUNK, l)
    row = lambda w: pl.BlockSpec((None, c, w), lambda i, n: (i, n, 0))
    whole = lambda shape: pl.BlockSpec(shape, lambda i, n: (0,) * len(shape))
    state = pl.BlockSpec((None, N_HEADS, HEAD_DIM, HEAD_DIM), lambda i, n: (i, 0, 0, 0))
    return pl.pallas_call(
        functools.partial(_delta_kernel, c, full_f32),
        grid=(b, l // c),
        in_specs=[row(QKV_WIDTH), row(HEADS_WIDTH), row(GATE_WIDTH),
                  pl.BlockSpec((None, DN_CONV - 1, QKV_WIDTH), lambda i, n: (i, 0, 0)), state,
                  whole((DN_CONV, QKV_WIDTH)), whole((1, GATE_WIDTH)), whole((1, GATE_WIDTH)),
                  whole((1, HEADS_WIDTH))],
        out_specs=[row(HEADS_WIDTH), state],
        out_shape=[jax.ShapeDtypeStruct((b, l, HEADS_WIDTH), F32),
                   jax.ShapeDtypeStruct((b, N_HEADS, HEAD_DIM, HEAD_DIM), F32)],
        scratch_shapes=[pltpu.VMEM((N_HEADS, HEAD_DIM, HEAD_DIM), F32),
                        pltpu.VMEM((8 + c, QKV_WIDTH), F32)],
        compiler_params=_params("arbitrary", "arbitrary"),
        name="delta_rule",
    )(ha, hz, hg, conv0, dn0, w_conv, a_log, dt_bias, norm_g)


PAIR_ROWS = 2 * CHUNK
N_PAIRS = N_HEADS // 2


def _attn_kernel(c, tq, mask_before_start, full_f32, q_ref, kp_ref, ks_ref, vp_ref, vs_ref, bias_ref, o_ref,
                 kwin_ref, vwin_ref):
    i = pl.program_id(1)
    prec = HIGHEST if full_f32 else None
    w = BAND_ROWS
    kwin_ref[0:w, :] = kp_ref[...]
    kwin_ref[w:w + tq, :] = ks_ref[...]
    vwin_ref[0:w, :] = vp_ref[...]
    vwin_ref[w:w + tq, :] = vs_ref[...]
    key_idx = lax.broadcasted_iota(jnp.int32, (1, w + c), 1)

    def chunk(j, carry):
        r0 = pl.multiple_of(j * c, c)
        q = q_ref[pl.ds(r0, c), :]
        kk = kwin_ref[pl.ds(r0, w + c), :]
        vv = vwin_ref[pl.ds(r0, w + c), :]
        valid = (i * tq + r0 + key_idx - w) >= 0
        for h in range(N_HEADS):
            sl = slice(h * HEAD_DIM, (h + 1) * HEAD_DIM)
            s = _dot_nt(q[:, sl], kk[:, sl], prec) * HEAD_DIM ** -0.5 + bias_ref[h]
            if mask_before_start:
                s = jnp.where(valid, s, -1e30)
            s = s - jnp.max(s, axis=-1, keepdims=True)
            p = jnp.exp(s)
            p = p / jnp.sum(p, axis=-1, keepdims=True)
            o_ref[pl.ds(r0, c), sl] = _dot(p, vv[:, sl], prec)
        return carry

    lax.fori_loop(0, tq // c, chunk, 0)


def _attention(q, k, v, k_hist, v_hist, bias, full_f32):
    b, l, _ = q.shape
    c = min(CHUNK, l)
    tq = min(ROW_TILE, l)
    assert tq == BAND_ROWS or l == tq
    own = pl.BlockSpec((None, tq, HEADS_WIDTH), lambda bi, i: (bi, i, 0))
    if k_hist is None:
        prev = pl.BlockSpec((None, BAND_ROWS, HEADS_WIDTH), lambda bi, i: (bi, jnp.maximum(i - 1, 0), 0))
        k_prev, v_prev = k, v
    else:
        prev = pl.BlockSpec((None, BAND_ROWS, HEADS_WIDTH), lambda bi, i: (bi, 0, 0))
        k_prev, v_prev = k_hist, v_hist
    return pl.pallas_call(
        functools.partial(_attn_kernel, c, tq, k_hist is None, full_f32),
        grid=(b, l // tq),
        in_specs=[own, prev, own, prev, own,
                  pl.BlockSpec((N_HEADS, c, BAND_ROWS + c), lambda bi, i: (0, 0, 0))],
        out_specs=own,
        out_shape=jax.ShapeDtypeStruct((b, l, HEADS_WIDTH), F32),
        scratch_shapes=[pltpu.VMEM((BAND_ROWS + tq, HEADS_WIDTH), F32),
                        pltpu.VMEM((BAND_ROWS + tq, HEADS_WIDTH), F32)],
        compiler_params=_params("arbitrary", "arbitrary"),
        name="band_attention",
    )(q, k_prev, k, v_prev, v, bias)


STAGED_ROWS = 256
STAGED_CHUNKS = STAGED_ROWS // CHUNK
STAGED_PROBLEMS = STAGED_CHUNKS * N_PAIRS
STAGED_BATCH = 2


def _delta_staged_kernel(qn_ref, kn_ref, v_ref, hz_ref, hg_ref, dn0_ref, alog_ref, dtb_ref, ng_ref,
                         tril_ref, pones_ref,
                         oa_ref, dn_ref,
                         z_ref, gc_ref, beta_ref, gct_ref,
                         pw_ref, sol_ref, a2_ref, ob_ref, qe_ref, kmat_ref, bsw_ref, egl_ref):
    n = pl.program_id(1)
    c, rows = CHUNK, STAGED_ROWS
    lane = lax.broadcasted_iota(jnp.int32, (1, LANES), 1)
    lo = lane < HEAD_DIM
    zero64 = jnp.zeros((HEAD_DIM, HEAD_DIM), F32)

    nb = STAGED_BATCH
    blocks = rows // LANES

    @pl.when(n == 0)
    def _():
        for bb in range(nb):
            for p in range(N_PAIRS):
                top = jnp.concatenate([zero64, dn0_ref[bb, 2 * p + 1]], axis=1)
                bot = jnp.concatenate([dn0_ref[bb, 2 * p], zero64], axis=1)
                z_ref[bb * N_PAIRS + p] = jnp.concatenate([top, bot], axis=0)

    neg_rate = jnp.where(lane < N_HEADS, -jnp.exp(alog_ref[...]), 0.0)
    for bb in range(nb):
        hg = hg_ref[bb]
        gc = _dot(tril_ref[...], neg_rate * jax.nn.softplus(hg + dtb_ref[...]), HIGHEST)
        gc_ref[bb] = gc
        beta_ref[bb] = jax.nn.sigmoid(hg)
        for blk in range(blocks):
            gct_ref[bb * blocks + blk] = gc[blk * LANES:(blk + 1) * LANES, :].T[0:8, :]

    row_i = lax.broadcasted_iota(jnp.int32, (c, 1), 0)
    col_j = lane % HEAD_DIM
    incl = row_i >= col_j
    strict = row_i > col_j
    pr = lax.broadcasted_iota(jnp.int32, (LANES, LANES), 0) // HEAD_DIM
    pc = lax.broadcasted_iota(jnp.int32, (LANES, LANES), 1) // HEAD_DIM
    anti = pr != pc
    swap = lambda x: jnp.concatenate([x[HEAD_DIM:], x[:HEAD_DIM]], axis=0)
    halves = lambda x: jnp.concatenate([jnp.where(lo, x, jnp.zeros_like(x)), jnp.where(lo, jnp.zeros_like(x), x)],
                                       axis=0)
    problems = [(bb, s, p) for bb in range(nb) for s in range(STAGED_CHUNKS) for p in range(N_PAIRS)]

    def gate_cols(bb, s, p):
        rs = slice(s * c, (s + 1) * c)
        e, o = 2 * p, 2 * p + 1
        gcs = gc_ref[bb, rs, :]
        g_e, g_o = gcs[:, e:e + 1], gcs[:, o:o + 1]
        return g_e, g_o, g_e[c - 1:c, :], g_o[c - 1:c, :]

    for gi, (bb, s, p) in enumerate(problems):
        rs = slice(s * c, (s + 1) * c)
        ls = slice(p * LANES, (p + 1) * LANES)
        e, o = 2 * p, 2 * p + 1
        kp, vp, qp = kn_ref[bb, rs, ls], v_ref[bb, rs, ls], qn_ref[bb, rs, ls]
        ksw = pltpu.roll(kp, HEAD_DIM, axis=1)
        g_e, g_o, _, _ = gate_cols(bb, s, p)
        betas = beta_ref[bb, rs, :]
        b_e, b_o = betas[:, N_HEADS + e:N_HEADS + e + 1], betas[:, N_HEADS + o:N_HEADS + o + 1]
        gt = gct_ref[bb * blocks + s // 2]
        gt_sw = pltpu.roll(gt, HEAD_DIM, axis=1)
        if s % 2 == 0:
            g_row = jnp.where(lo, gt[e:e + 1, :], gt_sw[o:o + 1, :])
        else:
            g_row = jnp.where(lo, gt_sw[e:e + 1, :], gt[o:o + 1, :])
        diff = jnp.where(lo, g_e, g_o) - g_row
        decay = jnp.where(incl, jnp.exp(jnp.where(incl, diff, 0.0)), 0.0)
        kb = kp * jnp.where(lo, b_e, b_o)
        st = _dot_nt(jnp.concatenate([kb, qp], axis=0).astype(BF16), halves(kp.astype(BF16)))
        pw_ref[gi] = jnp.where(strict, -(st[0:c] * decay), 0.0).astype(BF16)
        a2_ref[gi] = jnp.where(incl, st[c:2 * c] * decay, 0.0).astype(BF16)
        sol_ref[gi, :, 0:LANES] = jnp.where(lo, vp, ksw * jnp.exp(g_e)) * b_e
        sol_ref[gi, :, LANES:2 * LANES] = jnp.where(lo, ksw * jnp.exp(g_o), vp) * b_o

    zeros_cl = jnp.zeros((c, LANES), BF16)
    n_stage = int(math.log2(c))
    for stage in range(n_stage):
        tail = stage == n_stage - 1
        for gi in range(len(problems)):
            power = pw_ref[gi]
            sol = sol_ref[gi]
            sb = sol.astype(BF16)
            top = [sb[:, 0:LANES], zeros_cl] + ([] if tail else [jnp.where(lo, power, zeros_cl)])
            bot = [zeros_cl, sb[:, LANES:2 * LANES]] + ([] if tail else [jnp.where(lo, zeros_cl, power)])
            rhs = jnp.concatenate([jnp.concatenate(top, axis=1), jnp.concatenate(bot, axis=1)], axis=0)
            res = _dot(power, rhs)
            sol_ref[gi] = sol + res[:, 0:2 * LANES]
            if not tail:
                pw_ref[gi] = res[:, 2 * LANES:3 * LANES].astype(BF16)

    for gi, (bb, s, p) in enumerate(problems):
        rs = slice(s * c, (s + 1) * c)
        ls = slice(p * LANES, (p + 1) * LANES)
        sol = sol_ref[gi]
        sol_e, sol_o = sol[:, 0:LANES], sol[:, LANES:2 * LANES]
        g_e, g_o, gl_e, gl_o = gate_cols(bb, s, p)
        top = jnp.concatenate([jnp.where(lo, sol_e, 0.0), jnp.where(lo, 0.0, sol_e)], axis=1)
        bot = jnp.concatenate([jnp.where(lo, 0.0, sol_o), jnp.where(lo, sol_o, 0.0)], axis=1)
        y = _dot(a2_ref[gi], jnp.concatenate([top, bot], axis=0).astype(BF16))
        ob_ref[gi] = y[:, 0:LANES]
        qsw = pltpu.roll(qn_ref[bb, rs, ls], HEAD_DIM, axis=1)
        qe_ref[gi] = (qsw * jnp.where(lo, jnp.exp(g_o), jnp.exp(g_e)) - y[:, LANES:2 * LANES]).astype(BF16)
        kt = kn_ref[bb, rs, ls] * jnp.exp(jnp.where(lo, gl_e - g_e, gl_o - g_o))
        kbm = _dot_tn(halves(kt.astype(BF16)), jnp.concatenate([sol_e, sol_o], axis=0).astype(BF16))
        kmat_ref[gi] = jnp.where(anti, kbm, 0.0).astype(BF16)
        bsw_ref[gi] = swap(jnp.where(anti, 0.0, kbm))
        egl_ref[gi] = jnp.exp(jnp.where(lo, gl_e, gl_o))

    for s in range(STAGED_CHUNKS):
        rs = slice(s * c, (s + 1) * c)
        for bb in range(nb):
            for p in range(N_PAIRS):
                gi = (bb * STAGED_CHUNKS + s) * N_PAIRS + p
                zi = bb * N_PAIRS + p
                ls = slice(p * LANES, (p + 1) * LANES)
                zst = z_ref[zi]
                zb = zst.astype(BF16)
                out = ob_ref[gi] + _dot(qe_ref[gi], zb)
                z_ref[zi] = zst * egl_ref[gi] - swap(_dot(kmat_ref[gi], zb)) + bsw_ref[gi]
                ms = _dot((out * out).astype(BF16), pones_ref[...]) * (1.0 / HEAD_DIM)
                oa_ref[bb, rs, ls] = out * lax.rsqrt(ms + RMS_EPS) * ng_ref[:, ls] * _silu(hz_ref[bb, rs, ls])

    @pl.when(n == pl.num_programs(1) - 1)
    def _():
        for bb in range(nb):
            for p in range(N_PAIRS):
                zst = z_ref[bb * N_PAIRS + p]
                dn_ref[bb, 2 * p] = zst[HEAD_DIM:, 0:HEAD_DIM]
                dn_ref[bb, 2 * p + 1] = zst[0:HEAD_DIM, HEAD_DIM:]


def _head_ones():
    hi, hj = jnp.arange(HEADS_WIDTH)[:, None] // HEAD_DIM, jnp.arange(HEADS_WIDTH)[None, :] // HEAD_DIM
    return (hi == hj).astype(BF16)


def _delta_staged(qn, kn, v, hz, hg, dn0, a_log, dt_bias, norm_g):
    b, l, _ = qn.shape
    rows, c, nb = STAGED_ROWS, CHUNK, STAGED_BATCH
    g = nb * STAGED_PROBLEMS
    assert b % nb == 0 and l % rows == 0
    ri, ci = jnp.arange(rows)[:, None], jnp.arange(rows)[None, :]
    tril = ((ri >= ci) & (ri // c == ci // c)).astype(F32)
    pair_ones = _head_ones()[0:LANES, 0:LANES]
    row = lambda w: pl.BlockSpec((nb, rows, w), lambda i, n: (i, n, 0))
    whole = lambda shape: pl.BlockSpec(shape, lambda i, n: (0,) * len(shape))
    state = pl.BlockSpec((nb, N_HEADS, HEAD_DIM, HEAD_DIM), lambda i, n: (i, 0, 0, 0))
    return pl.pallas_call(
        _delta_staged_kernel,
        grid=(b // nb, l // rows),
        in_specs=[row(HEADS_WIDTH), row(HEADS_WIDTH), row(HEADS_WIDTH), row(HEADS_WIDTH), row(GATE_WIDTH), state,
                  whole((1, GATE_WIDTH)), whole((1, GATE_WIDTH)),
                  whole((1, HEADS_WIDTH)), whole((rows, rows)), whole((LANES, LANES))],
        out_specs=[row(HEADS_WIDTH), state],
        out_shape=[jax.ShapeDtypeStruct((b, l, HEADS_WIDTH), F32),
                   jax.ShapeDtypeStruct((b, N_HEADS, HEAD_DIM, HEAD_DIM), F32)],
        scratch_shapes=[pltpu.VMEM((nb * N_PAIRS, LANES, LANES), F32),
                        pltpu.VMEM((nb, rows, GATE_WIDTH), F32),
                        pltpu.VMEM((nb, rows, GATE_WIDTH), F32),
                        pltpu.VMEM((nb * rows // LANES, 8, LANES), F32),
                        pltpu.VMEM((g, c, LANES), BF16),
                        pltpu.VMEM((g, c, 2 * LANES), F32),
                        pltpu.VMEM((g, c, LANES), BF16),
                        pltpu.VMEM((g, c, LANES), F32),
                        pltpu.VMEM((g, c, LANES), BF16),
                        pltpu.VMEM((g, LANES, LANES), BF16),
                        pltpu.VMEM((g, LANES, LANES), F32),
                        pltpu.VMEM((g, 1, LANES), F32)],
        compiler_params=_params("arbitrary", "arbitrary"),
        name="delta_rule_staged",
    )(qn, kn, v, hz, hg, dn0, a_log, dt_bias, norm_g, tril, pair_ones)


ATTN_WINDOW = BAND_ROWS + PAIR_ROWS


def _attn_pairs_kernel(q_ref, kp_ref, ks_ref, vp_ref, vs_ref, bias_ref, o_ref, kwin_ref, vwin_ref):
    i = pl.program_id(1)
    w, tq = BAND_ROWS, ROW_TILE
    kwin_ref[0:w, :] = kp_ref[...]
    kwin_ref[w:w + tq, :] = ks_ref[...]
    vwin_ref[0:w, :] = vp_ref[...]
    vwin_ref[w:w + tq, :] = vs_ref[...]
    lo = lax.broadcasted_iota(jnp.int32, (1, LANES), 1) < HEAD_DIM
    key_idx = lax.broadcasted_iota(jnp.int32, (1, ATTN_WINDOW), 1)
    ones = jnp.ones((ATTN_WINDOW, LANES), BF16)

    def chunk_pair(mask_start, m, carry):
        r0 = pl.multiple_of(m * PAIR_ROWS, PAIR_ROWS)
        for p in range(N_PAIRS):
            ls = slice(p * LANES, (p + 1) * LANES)
            q = q_ref[pl.ds(r0, PAIR_ROWS), ls] * HEAD_DIM ** -0.5
            kk = kwin_ref[pl.ds(r0, ATTN_WINDOW), ls]
            vv = vwin_ref[pl.ds(r0, ATTN_WINDOW), ls]
            zero = jnp.zeros_like(q)
            q2 = jnp.concatenate([jnp.where(lo, q, zero), jnp.where(lo, zero, q)], axis=0)
            s = _dot_nt(q2, kk) + bias_ref[p]
            if mask_start:
                s = jnp.where(r0 + key_idx < w, -1e30, s)
            pexp = jnp.exp(s - jnp.max(s, axis=-1, keepdims=True)).astype(BF16)
            pv = _dot(pexp, jnp.concatenate([vv, ones], axis=1))
            pv = pv[:, 0:LANES] / pv[:, LANES:2 * LANES]
            o_ref[pl.ds(r0, PAIR_ROWS), ls] = jnp.where(lo, pv[0:PAIR_ROWS], pv[PAIR_ROWS:2 * PAIR_ROWS])
        return carry

    @pl.when(i == 0)
    def _():
        lax.fori_loop(0, tq // PAIR_ROWS, functools.partial(chunk_pair, True), 0, unroll=4)

    @pl.when(i > 0)
    def _():
        lax.fori_loop(0, tq // PAIR_ROWS, functools.partial(chunk_pair, False), 0, unroll=4)


def _pair_bias(rel_table):
    a = jnp.arange(PAIR_ROWS)[:, None]
    idx = jnp.arange(ATTN_WINDOW)[None, :]
    first = (a // CHUNK) * CHUNK
    in_band = (idx >= first) & (idx < first + BAND_ROWS + CHUNK)
    bias = jnp.where(in_band[None], _rel_bias(rel_table, PAIR_ROWS), -1e30)
    return bias.reshape(N_PAIRS, 2 * PAIR_ROWS, ATTN_WINDOW)


def _attention_pairs(qkv, bias):
    b, l, _ = qkv.shape
    tq = ROW_TILE
    assert tq == BAND_ROWS and l % tq == 0
    own = lambda j: pl.BlockSpec((None, tq, HEADS_WIDTH), lambda bi, i: (bi, i, j))
    prev = lambda j: pl.BlockSpec((None, BAND_ROWS, HEADS_WIDTH), lambda bi, i: (bi, jnp.maximum(i - 1, 0), j))
    return pl.pallas_call(
        _attn_pairs_kernel,
        grid=(b, l // tq),
        in_specs=[own(0), prev(1), own(1), prev(2), own(2),
                  pl.BlockSpec((N_PAIRS, 2 * PAIR_ROWS, ATTN_WINDOW), lambda bi, i: (0, 0, 0))],
        out_specs=own(0),
        out_shape=jax.ShapeDtypeStruct((b, l, HEADS_WIDTH), F32),
        scratch_shapes=[pltpu.VMEM((BAND_ROWS + tq, HEADS_WIDTH), BF16),
                        pltpu.VMEM((BAND_ROWS + tq, HEADS_WIDTH), BF16)],
        compiler_params=_params("arbitrary", "arbitrary"),
        name="band_attention_pairs",
    )(qkv, qkv, qkv, qkv, qkv, bias)


def _mix_kernel(tile, start, full_f32, oa_ref, ob_ref, hc_ref, x_ref, pool0_ref, pw_ref, ps_ref, wout_ref, g_ref, b_ref,
                x1_ref, ext_ref):
    i = pl.program_id(1)
    prec = HIGHEST if full_f32 else None
    wdt = wout_ref.dtype
    hist0 = 16 - POOL_HIST

    @pl.when(i == 0)
    def _():
        ext_ref[0:hist0, :] = jnp.zeros((hist0, POOL_WIDTH), F32)
        ext_ref[hist0:16, :] = pool0_ref[...]

    u = hc_ref[...]
    ext_ref[16:16 + tile, :] = u
    window = 2 << (lax.broadcasted_iota(jnp.int32, (1, POOL_WIDTH), 1) // POOL_GROUP_DIM)
    level = ext_ref[...]
    sums = {}
    for wdw in POOL_WINDOWS:
        level = level + pltpu.roll(level, wdw // 2, axis=0)
        sums[wdw] = level[16:16 + tile, :]
    wsum = sums[POOL_WINDOWS[-1]]
    for wdw in reversed(POOL_WINDOWS[:-1]):
        wsum = jnp.where(window == wdw, sums[wdw], wsum)
    ext_ref[hist0:16, :] = u[tile - POOL_HIST:tile, :]
    pos1 = start + i * tile + 1 + lax.broadcasted_iota(jnp.int32, (tile, 1), 0)
    cnt = jnp.minimum(pos1, window).astype(F32)
    res = wsum / cnt - u
    oc = _dot(res, pw_ref[...], prec) * ps_ref[...]

    mix = _dot(oa_ref[...].astype(wdt), wout_ref[0:HEADS_WIDTH, :], prec)
    mix = mix + _dot(ob_ref[...].astype(wdt), wout_ref[HEADS_WIDTH:2 * HEADS_WIDTH, :], prec)
    mix = mix + _dot(oc.astype(wdt), wout_ref[2 * HEADS_WIDTH:D_MODEL, :], prec)
    x1_ref[...] = _layer_norm(DEEPNORM_ALPHA * x_ref[...] + mix, g_ref[...], b_ref[...])


def _mix(oa, ob, hc, x, pool0, pool_wbd, pool_scale, w_out, ln_g, ln_b, start, full_f32):
    b, l, _ = x.shape
    tile = min(ROW_TILE, l)
    row = lambda w: pl.BlockSpec((None, tile, w), lambda bi, i: (bi, i, 0))
    whole = lambda shape: pl.BlockSpec(shape, lambda bi, i: (0,) * len(shape))
    return pl.pallas_call(
        functools.partial(_mix_kernel, tile, start, full_f32),
        grid=(b, l // tile),
        in_specs=[row(HEADS_WIDTH), row(HEADS_WIDTH), row(POOL_WIDTH), row(D_MODEL),
                  pl.BlockSpec((None, POOL_HIST, POOL_WIDTH), lambda bi, i: (bi, 0, 0)),
                  whole((POOL_WIDTH, POOL_WIDTH)), whole((1, POOL_WIDTH)), whole((D_MODEL, D_MODEL)),
                  whole((1, D_MODEL)), whole((1, D_MODEL))],
        out_specs=row(D_MODEL),
        out_shape=jax.ShapeDtypeStruct((b, l, D_MODEL), F32),
        scratch_shapes=[pltpu.VMEM((16 + tile, POOL_WIDTH), F32)],
        compiler_params=_params("arbitrary", "arbitrary"),
        name="pool_outproj_ln1",
    )(oa, ob, hc, x, pool0, pool_wbd, pool_scale, w_out, ln_g, ln_b)


def _top2_of4(a, b, c, d):
    hi1, lo1 = jnp.maximum(a, b), jnp.minimum(a, b)
    hi2, lo2 = jnp.maximum(c, d), jnp.minimum(c, d)
    return jnp.maximum(hi1, hi2), jnp.maximum(jnp.minimum(hi1, hi2), jnp.maximum(lo1, lo2))


def _route(logits_t, bias_t):
    aff = jax.nn.sigmoid(logits_t[0:N_EXPERTS, :])
    sel = aff + bias_t[0:N_EXPERTS, :]
    t = sel.shape[1]
    scores = []
    for gi in range(N_EXPERT_GROUPS):
        r = [sel[4 * gi + m:4 * gi + m + 1, :] for m in range(EXPERTS_PER_GROUP)]
        top1, top2 = _top2_of4(*r)
        scores.append(top1 + top2)
    best = scores[0]
    best_g = jnp.zeros((1, t), jnp.int32)
    for gi in range(1, N_EXPERT_GROUPS):
        better = scores[gi] > best
        best = jnp.where(better, scores[gi], best)
        best_g = jnp.where(better, gi, best_g)
    e_idx = lax.broadcasted_iota(jnp.int32, (N_EXPERTS, t), 0)
    masked = jnp.where(e_idx // EXPERTS_PER_GROUP == best_g, sel, -jnp.inf)
    m1 = jnp.max(masked, axis=0, keepdims=True)
    i1 = jnp.min(jnp.where(masked == m1, e_idx, N_EXPERTS), axis=0, keepdims=True)
    rest = jnp.where(e_idx == i1, -jnp.inf, masked)
    m2 = jnp.max(rest, axis=0, keepdims=True)
    i2 = jnp.min(jnp.where(rest == m2, e_idx, N_EXPERTS), axis=0, keepdims=True)
    w1 = jnp.sum(jnp.where(e_idx == i1, aff, 0.0), axis=0, keepdims=True)
    w2 = jnp.sum(jnp.where(e_idx == i2, aff, 0.0), axis=0, keepdims=True)
    tot = w1 + w2
    gates = jnp.where(e_idx == i1, w1 / tot, 0.0) + jnp.where(e_idx == i2, w2 / tot, 0.0)
    return jnp.concatenate([gates, jnp.zeros((LANES - N_EXPERTS, t), F32)], axis=0), best_g


def _expert(xb, w_gu, w_d, gates, e, prec):
    gu = _dot(xb, w_gu, prec)
    hid = _silu(gu[:, 0:D_EXPERT]) * gu[:, D_EXPERT:2 * D_EXPERT]
    lane = lax.broadcasted_iota(jnp.int32, (1, LANES), 1)
    ge = jnp.sum(jnp.where(lane == e, gates, 0.0), axis=-1, keepdims=True)
    return _dot(hid.astype(w_d.dtype), w_d, prec) * ge


def _ffn_kernel(x_ref, wr_ref, rb_ref, wgu_ref, wd_ref, g_ref, b_ref, y_ref):
    x = x_ref[...]
    xb = x.astype(BF16)
    logits = _dot(xb, wr_ref[...])
    gates = _route(logits.T, rb_ref[...])[0].T
    acc = jnp.zeros(x.shape, F32)
    for e in range(N_EXPERTS):
        acc = acc + _expert(xb, wgu_ref[e], wd_ref[e], gates, e, None)
    y_ref[...] = _layer_norm(DEEPNORM_ALPHA * x + acc, g_ref[...], b_ref[...])


def _ffn_f32_kernel(x_ref, wr_ref, rb_ref, wgu_ref, wd_ref, g_ref, b_ref, y_ref, gates_ref, acc_ref):
    e = pl.program_id(1)
    x = x_ref[...]

    @pl.when(e == 0)
    def _():
        logits = _dot(x, wr_ref[...], HIGHEST)
        gates_ref[...] = _route(logits.T, rb_ref[...])[0].T
        acc_ref[...] = jnp.zeros(acc_ref.shape, F32)

    acc_ref[...] += _expert(x, wgu_ref[...], wd_ref[...], gates_ref[...], e, HIGHEST)

    @pl.when(e == N_EXPERTS - 1)
    def _():
        y_ref[...] = _layer_norm(DEEPNORM_ALPHA * x + acc_ref[...], g_ref[...], b_ref[...])


def _ffn(x2d, w_router, router_bias, w_gu, w_down, ln_g, ln_b, full_f32):
    t = x2d.shape[0]
    tm = min(ROW_TILE, t)
    out_shape = jax.ShapeDtypeStruct((t, D_MODEL), F32)
    if full_f32:
        whole = lambda shape: pl.BlockSpec(shape, lambda i, e: (0,) * len(shape))
        return pl.pallas_call(
            _ffn_f32_kernel,
            grid=(t // tm, N_EXPERTS),
            in_specs=[pl.BlockSpec((tm, D_MODEL), lambda i, e: (i, 0)),
                      whole((D_MODEL, LANES)), whole((LANES, 1)),
                      pl.BlockSpec((None, D_MODEL, 2 * D_EXPERT), lambda i, e: (e, 0, 0)),
                      pl.BlockSpec((None, D_EXPERT, D_MODEL), lambda i, e: (e, 0, 0)),
                      whole((1, D_MODEL)), whole((1, D_MODEL))],
            out_specs=pl.BlockSpec((tm, D_MODEL), lambda i, e: (i, 0)),
            out_shape=out_shape,
            scratch_shapes=[pltpu.VMEM((tm, LANES), F32), pltpu.VMEM((tm, D_MODEL), F32)],
            compiler_params=_params("arbitrary", "arbitrary"),
            name="routed_ffn_ln2_f32",
        )(x2d, w_router, router_bias, w_gu, w_down, ln_g, ln_b)
    whole = lambda shape: pl.BlockSpec(shape, lambda i: (0,) * len(shape), pipeline_mode=pl.Buffered(1))
    return pl.pallas_call(
        _ffn_kernel,
        grid=(t // tm,),
        in_specs=[pl.BlockSpec((tm, D_MODEL), lambda i: (i, 0)),
                  whole((D_MODEL, LANES)), whole((LANES, 1)),
                  whole((N_EXPERTS, D_MODEL, 2 * D_EXPERT)), whole((N_EXPERTS, D_EXPERT, D_MODEL)),
                  whole((1, D_MODEL)), whole((1, D_MODEL))],
        out_specs=pl.BlockSpec((tm, D_MODEL), lambda i: (i, 0)),
        out_shape=out_shape,
        compiler_params=_params("arbitrary"),
        name="routed_ffn_ln2",
    )(x2d, w_router, router_bias, w_gu, w_down, ln_g, ln_b)


def _prep_layer(l, wdt, w_in, w_conv, a_log, dt_bias, dn_norm_g, rel_table, pool_w, pool_scale, w_out,
                ln1_g, ln1_b, w_gate, w_up, w_down, ln2_g, ln2_b):
    wi = w_in[l]
    gates = jnp.zeros((D_MODEL, GATE_WIDTH), F32).at[:, 0:2 * N_HEADS].set(wi[:, OFF_AA:OFF_BQ])
    w_perm = jnp.concatenate([wi[:, OFF_AQ:OFF_AZ], wi[:, OFF_AZ:OFF_AA], wi[:, OFF_BQ:OFF_CU],
                              wi[:, OFF_CU:IN_WIDTH], gates], axis=1).astype(wdt)
    pad_heads = lambda vec: jnp.zeros((1, GATE_WIDTH), F32).at[0, 0:N_HEADS].set(vec)
    pool_wbd = jnp.zeros((POOL_WIDTH, POOL_WIDTH), F32)
    for gi in range(len(POOL_WINDOWS)):
        s = slice(gi * POOL_GROUP_DIM, (gi + 1) * POOL_GROUP_DIM)
        pool_wbd = pool_wbd.at[s, s].set(pool_w[l, gi])
    return dict(
        w_perm=w_perm, w_conv=w_conv[l], a_log=pad_heads(a_log[l]), dt_bias=pad_heads(dt_bias[l]),
        norm_g=jnp.tile(dn_norm_g[l], N_HEADS)[None, :], rel_table=rel_table[l],
        pool_wbd=pool_wbd, pool_scale=pool_scale[l][None, :], w_out=w_out[l].astype(wdt),
        ln1_g=ln1_g[l][None, :], ln1_b=ln1_b[l][None, :],
        w_gu=jnp.concatenate([w_gate[l], w_up[l]], axis=-1).astype(wdt), w_down=w_down[l].astype(wdt),
        ln2_g=ln2_g[l][None, :], ln2_b=ln2_b[l][None, :])


def _rel_bias(rel_table, c):
    w = BAND_ROWS + c
    period = c + w
    m = jnp.arange(period)
    delta = jnp.where(m < w, -m, period - m)
    vec = rel_table[:, jnp.clip(delta + BAND_ROWS, -REL_CLIP, REL_CLIP) + REL_CLIP].astype(F32)
    rolled = jnp.tile(vec, (1, c))[:, :c * (period - 1)].reshape(rel_table.shape[0], c, period - 1)
    return rolled[:, :, :w]


def _last_rows(t, n):
    if t.shape[1] < n:
        t = jnp.concatenate([jnp.zeros((t.shape[0], n - t.shape[1]) + t.shape[2:], t.dtype), t], axis=1)
    return t[:, t.shape[1] - n:]


def _trunk(x, start, dn_state, conv_state, k_hist, v_hist, pool_state, band_rows, layers, w_router, router_bias,
           full_f32):
    b, l, _ = x.shape
    assert l >= POOL_HIST
    c = min(CHUNK, l)
    new_dn, new_conv, new_k, new_v, new_pool = [], [], [], [], []
    for li, p in enumerate(layers):
        ha, hz, hb, hc, hg = _inproj(x.reshape(b * l, D_MODEL), p["w_perm"], full_f32)
        ha = ha.reshape(b, l, QKV_WIDTH)
        hz = hz.reshape(b, l, HEADS_WIDTH)
        hb = hb.reshape(b, l, QKV_WIDTH)
        hc = hc.reshape(b, l, POOL_WIDTH)
        hg = hg.reshape(b, l, GATE_WIDTH)
        oa, dn_new = _delta(ha, hz, hg, conv_state[li], dn_state[li], p["w_conv"], p["a_log"], p["dt_bias"],
                            p["norm_g"], full_f32)
        qb, kb, vb = (hb[..., j * HEADS_WIDTH:(j + 1) * HEADS_WIDTH] for j in range(3))
        bias = _rel_bias(p["rel_table"], c)
        if k_hist is None:
            ob = _attention(qb, kb, vb, None, None, bias, full_f32)
            k_new, v_new = _last_rows(kb, band_rows), _last_rows(vb, band_rows)
        else:
            kh = k_hist[li].reshape(b, -1, HEADS_WIDTH)
            vh = v_hist[li].reshape(b, -1, HEADS_WIDTH)
            ob = _attention(qb, kb, vb, kh, vh, bias, full_f32)
            k_new = _last_rows(jnp.concatenate([kh, kb], axis=1), band_rows)
            v_new = _last_rows(jnp.concatenate([vh, vb], axis=1), band_rows)
        x1 = _mix(oa, ob, hc, x, pool_state[li], p["pool_wbd"], p["pool_scale"], p["w_out"], p["ln1_g"],
                  p["ln1_b"], start, full_f32)
        x = _ffn(x1.reshape(b * l, D_MODEL), w_router.astype(p["w_out"].dtype), router_bias, p["w_gu"],
                 p["w_down"], p["ln2_g"], p["ln2_b"], full_f32).reshape(b, l, D_MODEL)
        new_dn.append(dn_new)
        new_conv.append(ha[:, l - (DN_CONV - 1):])
        new_k.append(k_new.reshape(b, band_rows, N_HEADS, HEAD_DIM))
        new_v.append(v_new.reshape(b, band_rows, N_HEADS, HEAD_DIM))
        new_pool.append(hc[:, l - POOL_HIST:])
    return (x, jnp.stack(new_dn), jnp.stack(new_conv), jnp.stack(new_k), jnp.stack(new_v), jnp.stack(new_pool))


def _trunk_seq(x, dn_state, conv_state, pool_state, band_rows, layers, w_router, router_bias):
    b, l, _ = x.shape
    assert band_rows == BAND_ROWS and l >= band_rows
    new_dn, new_conv, new_k, new_v, new_pool = [], [], [], [], []
    for li, p in enumerate(layers):
        qn, kn, va, hz, hb, hc, hg, kv_last, conv_new = _inproj_seq(x, p["w_perm"], conv_state[li], p["w_conv"])
        oa, dn_new = _delta_staged(qn, kn, va, hz, hg, dn_state[li], p["a_log"], p["dt_bias"], p["norm_g"])
        ob = _attention_pairs(hb, _pair_bias(p["rel_table"]))
        x1 = _mix(oa, ob, hc, x, pool_state[li], p["pool_wbd"], p["pool_scale"], p["w_out"], p["ln1_g"],
                  p["ln1_b"], 0, False)
        x = _ffn(x1.reshape(b * l, D_MODEL), w_router.astype(BF16), router_bias, p["w_gu"], p["w_down"],
                 p["ln2_g"], p["ln2_b"], False).reshape(b, l, D_MODEL)
        new_dn.append(dn_new)
        new_conv.append(conv_new)
        new_k.append(kv_last[..., 0:HEADS_WIDTH].reshape(b, band_rows, N_HEADS, HEAD_DIM))
        new_v.append(kv_last[..., HEADS_WIDTH:].reshape(b, band_rows, N_HEADS, HEAD_DIM))
        new_pool.append(hc[:, l - POOL_HIST:])
    return (x, jnp.stack(new_dn), jnp.stack(new_conv), jnp.stack(new_k), jnp.stack(new_v), jnp.stack(new_pool))


@jax.jit
def kernel(x_prompt, x_sample, state_dn, state_conv, cache_k, cache_v, state_pool, w_in, w_conv, a_log, dt_bias,
           dn_norm_g, rel_table, pool_w, pool_scale, w_out, ln1_g, ln1_b, w_router, router_bias, w_gate, w_up,
           w_down, ln2_g, ln2_b):
    weights = (w_in, w_conv, a_log, dt_bias, dn_norm_g, rel_table, pool_w, pool_scale, w_out,
               ln1_g, ln1_b, w_gate, w_up, w_down, ln2_g, ln2_b)
    layers_bf16 = [_prep_layer(l, BF16, *weights) for l in range(DEPTH)]
    layers_f32 = [_prep_layer(l, F32, *weights) for l in range(DEPTH)]
    w_router_p = jnp.zeros((D_MODEL, LANES), F32).at[:, 0:N_EXPERTS].set(w_router)
    router_bias_p = jnp.zeros((LANES, 1), F32).at[0:N_EXPERTS, 0].set(router_bias)
    band_rows = cache_k.shape[2]
    bp = x_prompt.shape[0]
    zero_dn = jnp.zeros((DEPTH, bp, N_HEADS, HEAD_DIM, HEAD_DIM), state_dn.dtype)
    zero_conv = jnp.zeros((DEPTH, bp, DN_CONV - 1, QKV_WIDTH), x_prompt.dtype)
    zero_pool = jnp.zeros((DEPTH, bp, POOL_HIST, POOL_WIDTH), x_prompt.dtype)
    prompt = _trunk_seq(x_prompt, zero_dn, zero_conv, zero_pool, band_rows, layers_bf16, w_router_p, router_bias_p)
    sample = _trunk(x_sample, PAST_LEN, state_dn, state_conv, cache_k, cache_v, state_pool, band_rows, layers_f32,
                    w_router_p, router_bias_p, True)
    return (prompt[0], sample[0]) + prompt[1:] + sample[1:]
```

```python
import functools
import math

import jax
import jax.numpy as jnp
from jax import lax
from jax.experimental import pallas as pl
from jax.experimental.pallas import tpu as pltpu

F32 = jnp.float32
BF16 = jnp.bfloat16
HIGHEST = lax.Precision.HIGHEST

D_MODEL = 1024
HEAD_DIM = 64
N_HEADS = 6
HEADS_WIDTH = N_HEADS * HEAD_DIM
QKV_WIDTH = 3 * HEADS_WIDTH
DN_CONV = 4
CHUNK = 64
BAND_ROWS = 512
REL_CLIP = 128
POOL_WINDOWS = (2, 4, 8, 16)
POOL_GROUP_DIM = 64
POOL_WIDTH = 256
POOL_HIST = 15
N_EXPERTS = 16
N_EXPERT_GROUPS = 4
EXPERTS_PER_GROUP = 4
D_EXPERT = 256
DEPTH = 2
DEEPNORM_ALPHA = (2 * DEPTH) ** 0.25
LN_EPS = 1e-5
RMS_EPS = 1e-6
PAST_LEN = 1024

OFF_AQ = 0
OFF_AZ = 3 * HEADS_WIDTH
OFF_AA = OFF_AZ + HEADS_WIDTH
OFF_AB = OFF_AA + N_HEADS
OFF_BQ = OFF_AB + N_HEADS
OFF_CU = OFF_BQ + 3 * HEADS_WIDTH
IN_WIDTH = OFF_CU + POOL_WIDTH

LANES = 128
GATE_WIDTH = LANES
SEG_A = (0, QKV_WIDTH)
SEG_Z = (SEG_A[1], SEG_A[1] + HEADS_WIDTH)
SEG_B = (SEG_Z[1], SEG_Z[1] + QKV_WIDTH)
SEG_C = (SEG_B[1], SEG_B[1] + POOL_WIDTH)
SEG_G = (SEG_C[1], SEG_C[1] + GATE_WIDTH)
PERM_WIDTH = SEG_G[1]

ROW_TILE = 512
VMEM_LIMIT = 56 * 1024 * 1024


def _params(*sem):
    return pltpu.CompilerParams(dimension_semantics=sem, vmem_limit_bytes=VMEM_LIMIT)


def _dot(a, b, precision=None):
    return jnp.dot(a, b, preferred_element_type=F32, precision=precision)


def _dot_nt(a, b, precision=None):
    return lax.dot_general(a, b, (((1,), (1,)), ((), ())), preferred_element_type=F32, precision=precision)


def _dot_tn(a, b, precision=None):
    return lax.dot_general(a, b, (((0,), (0,)), ((), ())), preferred_element_type=F32, precision=precision)


def _silu(x):
    return x * jax.nn.sigmoid(x)


def _layer_norm(x, g, b):
    mu = jnp.mean(x, axis=-1, keepdims=True)
    xc = x - mu
    var = jnp.mean(xc * xc, axis=-1, keepdims=True)
    return xc * lax.rsqrt(var + LN_EPS) * g + b


def _inproj_kernel(full_f32, x_ref, w_ref, ha_ref, hz_ref, hb_ref, hc_ref, hg_ref):
    prec = HIGHEST if full_f32 else None
    xb = x_ref[...].astype(w_ref.dtype)
    for (lo, hi), out_ref in ((SEG_A, ha_ref), (SEG_Z, hz_ref), (SEG_B, hb_ref), (SEG_C, hc_ref), (SEG_G, hg_ref)):
        step = 384 if (hi - lo) % 384 == 0 else hi - lo
        for c0 in range(0, hi - lo, step):
            out_ref[:, c0:c0 + step] = _dot(xb, w_ref[:, lo + c0:lo + c0 + step], prec)


def _inproj(x2d, w_perm, full_f32):
    t = x2d.shape[0]
    tm = min(ROW_TILE, t)
    widths = [s[1] - s[0] for s in (SEG_A, SEG_Z, SEG_B, SEG_C, SEG_G)]
    return pl.pallas_call(
        functools.partial(_inproj_kernel, full_f32),
        grid=(t // tm,),
        in_specs=[pl.BlockSpec((tm, D_MODEL), lambda i: (i, 0)),
                  pl.BlockSpec((D_MODEL, PERM_WIDTH), lambda i: (0, 0))],
        out_specs=[pl.BlockSpec((tm, w), lambda i: (i, 0)) for w in widths],
        out_shape=[jax.ShapeDtypeStruct((t, w), F32) for w in widths],
        compiler_params=_params("arbitrary"),
        name="inproj",
    )(x2d, w_perm)


INPROJ_COLS = 512


def _inproj_seq_kernel(x_ref, w_ref, conv0_ref, wconv_ref, hones_ref,
                       qn_ref, kn_ref, v_ref, hz_ref, hb_ref, hc_ref, hg_ref, kv_ref, cs_ref, ext_ref):
    i = pl.program_id(1)
    tm = x_ref.shape[0]
    pad = 8 - (DN_CONV - 1)

    @pl.when(i == 0)
    def _():
        ext_ref[pad:8, :] = conv0_ref[...]

    xb = x_ref[...].astype(BF16)
    segments = ((SEG_Z, hz_ref, 0), (SEG_B, hb_ref, 0), (SEG_C, hc_ref, 0), (SEG_G, hg_ref, 0), (SEG_A, ext_ref, 8))
    kv0 = SEG_B[0] + HEADS_WIDTH
    for c0 in range(0, PERM_WIDTH, INPROJ_COLS):
        h = _dot(xb, w_ref[:, c0:c0 + INPROJ_COLS])
        for (lo, hi), out_ref, row0 in segments:
            a, b = max(lo, c0), min(hi, c0 + INPROJ_COLS)
            if a < b:
                out_ref[row0:row0 + tm, a - lo:b - lo] = h[:, a - c0:b - c0].astype(out_ref.dtype)
        a, b = max(kv0, c0), min(SEG_B[1], c0 + INPROJ_COLS)
        if a < b:
            kv_ref[:, a - kv0:b - kv0] = h[:, a - c0:b - c0]

    for r0 in range(0, tm, LANES):
        for part, dst in enumerate((qn_ref, kn_ref, v_ref)):
            cs = slice(part * HEADS_WIDTH, (part + 1) * HEADS_WIDTH)
            conv = ext_ref[pad + r0:pad + r0 + LANES, cs] * wconv_ref[0:1, cs]
            for j in range(1, DN_CONV):
                conv = conv + ext_ref[pad + j + r0:pad + j + r0 + LANES, cs] * wconv_ref[j:j + 1, cs]
            act = _silu(conv)
            if part < 2:
                inv = lax.rsqrt(_dot((act * act).astype(BF16), hones_ref[...]) + RMS_EPS)
                act = act * (inv * HEAD_DIM ** -0.5 if part == 0 else inv)
            dst[r0:r0 + LANES, :] = act
    last = ext_ref[8 + tm - (DN_CONV - 1):8 + tm, :]
    cs_ref[...] = last
    ext_ref[pad:8, :] = last


def _inproj_seq(x, w_perm, conv0, w_conv):
    b, l, _ = x.shape
    tm = ROW_TILE
    assert tm == BAND_ROWS and l % tm == 0
    widths = [HEADS_WIDTH] * 4 + [s[1] - s[0] for s in (SEG_B, SEG_C, SEG_G)]
    dtypes = [F32, F32, F32, F32, BF16, F32, F32]
    row = lambda w: pl.BlockSpec((None, tm, w), lambda bi, i: (bi, i, 0))
    whole = lambda shape: pl.BlockSpec(shape, lambda bi, i: (0,) * len(shape))
    conv_rows = pl.BlockSpec((None, DN_CONV - 1, QKV_WIDTH), lambda bi, i: (bi, 0, 0))
    return pl.pallas_call(
        _inproj_seq_kernel,
        grid=(b, l // tm),
        in_specs=[row(D_MODEL), whole((D_MODEL, PERM_WIDTH)), conv_rows, whole((DN_CONV, QKV_WIDTH)),
                  whole((HEADS_WIDTH, HEADS_WIDTH))],
        out_specs=[row(w) for w in widths]
        + [pl.BlockSpec((None, tm, 2 * HEADS_WIDTH), lambda bi, i: (bi, 0, 0)), conv_rows],
        out_shape=[jax.ShapeDtypeStruct((b, l, w), d) for w, d in zip(widths, dtypes)]
        + [jax.ShapeDtypeStruct((b, tm, 2 * HEADS_WIDTH), F32),
           jax.ShapeDtypeStruct((b, DN_CONV - 1, QKV_WIDTH), F32)],
        scratch_shapes=[pltpu.VMEM((8 + tm, QKV_WIDTH), F32)],
        compiler_params=_params("arbitrary", "arbitrary"),
        name="inproj_seq",
    )(x, w_perm, conv0, w_conv, _head_ones())


def _delta_kernel(c, full_f32, ha_ref, hz_ref, hg_ref, conv0_ref, dn0_ref, wconv_ref, alog_ref, dtb_ref, ng_ref,
                  oa_ref, dn_ref, s_ref, ext_ref):
    n = pl.program_id(1)
    prec = HIGHEST if full_f32 else None
    pad = 8 - (DN_CONV - 1)

    @pl.when(n == 0)
    def _():
        s_ref[...] = dn0_ref[...]
        ext_ref[pad:8, :] = conv0_ref[...]

    u = ha_ref[...]
    ext_ref[8:8 + c, :] = u
    conv = ext_ref[pad:pad + c, :] * wconv_ref[0:1, :]
    for j in range(1, DN_CONV):
        conv = conv + ext_ref[pad + j:pad + j + c, :] * wconv_ref[j:j + 1, :]
    ext_ref[pad:8, :] = u[c - (DN_CONV - 1):c, :]
    qkv = _silu(conv)
    q = qkv[:, 0:HEADS_WIDTH]
    k = qkv[:, HEADS_WIDTH:2 * HEADS_WIDTH]
    v = qkv[:, 2 * HEADS_WIDTH:QKV_WIDTH]

    r = lax.broadcasted_iota(jnp.int32, (HEADS_WIDTH, HEADS_WIDTH), 0) // HEAD_DIM
    cc = lax.broadcasted_iota(jnp.int32, (HEADS_WIDTH, HEADS_WIDTH), 1) // HEAD_DIM
    head_ones = (r == cc).astype(F32)
    qn = q * lax.rsqrt(_dot(q * q, head_ones, HIGHEST) + RMS_EPS) * HEAD_DIM ** -0.5
    kn = k * lax.rsqrt(_dot(k * k, head_ones, HIGHEST) + RMS_EPS)

    hg = hg_ref[...]
    lane = lax.broadcasted_iota(jnp.int32, (1, GATE_WIDTH), 1)
    neg_rate = jnp.where(lane < N_HEADS, -jnp.exp(alog_ref[...]), 0.0)
    g = neg_rate * jax.nn.softplus(hg + dtb_ref[...])
    beta = jax.nn.sigmoid(hg)
    ri = lax.broadcasted_iota(jnp.int32, (c, c), 0)
    ci = lax.broadcasted_iota(jnp.int32, (c, c), 1)
    incl = ri >= ci
    strict = ri > ci
    gc = _dot(incl.astype(F32), g, HIGHEST)
    gc_t = _dot_tn(g, (ri <= ci).astype(F32), HIGHEST)

    z = hz_ref[...]
    for h in range(N_HEADS):
        sl = slice(h * HEAD_DIM, (h + 1) * HEAD_DIM)
        qh, kh, vh = qn[:, sl], kn[:, sl], v[:, sl]
        gch = gc[:, h:h + 1]
        diff = gch - gc_t[h:h + 1, :]
        decay = jnp.where(incl, jnp.exp(jnp.where(incl, diff, 0.0)), 0.0)
        bh = beta[:, N_HEADS + h:N_HEADS + h + 1]
        kbeta = kh * bh
        eg = jnp.exp(gch)
        neg_m = jnp.where(strict, -(_dot_nt(kbeta, kh, prec) * decay), 0.0)
        sol = jnp.concatenate([vh * bh, kbeta * eg], axis=-1)
        power = neg_m
        sol = sol + _dot(power, sol, prec)
        for _ in range(int(math.log2(c)) - 1):
            power = _dot(power, power, prec)
            sol = sol + _dot(power, sol, prec)
        u_h, w_h = sol[:, 0:HEAD_DIM], sol[:, HEAD_DIM:2 * HEAD_DIM]
        a_intra = jnp.where(incl, _dot_nt(qh, kh, prec) * decay, 0.0)
        g_last = gc[c - 1:c, h:h + 1]
        k_tail = kh * jnp.exp(g_last - gch)
        s = s_ref[h]
        v_new = u_h - _dot(w_h, s, prec)
        o = _dot(qh * eg, s, prec) + _dot(a_intra, v_new, prec)
        s_ref[h] = s * jnp.exp(g_last) + _dot_tn(k_tail, v_new, prec)
        o = o * lax.rsqrt(jnp.mean(o * o, axis=-1, keepdims=True) + RMS_EPS)
        oa_ref[:, sl] = o * ng_ref[:, sl] * _silu(z[:, sl])

    @pl.when(n == pl.num_programs(1) - 1)
    def _():
        dn_ref[...] = s_ref[...]


def _delta(ha, hz, hg, conv0, dn0, w_conv, a_log, dt_bias, norm_g, full_f32):
    b, l, _ = ha.shape
    c = min(CHUNK, l)
    row = lambda w: pl.BlockSpec((None, c, w), lambda i, n: (i, n, 0))
    whole = lambda shape: pl.BlockSpec(shape, lambda i, n: (0,) * len(shape))
    state = pl.BlockSpec((None, N_HEADS, HEAD_DIM, HEAD_DIM), lambda i, n: (i, 0, 0, 0))
    return pl.pallas_call(
        functools.partial(_delta_kernel, c, full_f32),
        grid=(b, l // c),
        in_specs=[row(QKV_WIDTH), row(HEADS_WIDTH), row(GATE_WIDTH),
                  pl.BlockSpec((None, DN_CONV - 1, QKV_WIDTH), lambda i, n: (i, 0, 0)), state,
                  whole((DN_CONV, QKV_WIDTH)), whole((1, GATE_WIDTH)), whole((1, GATE_WIDTH)),
                  whole((1, HEADS_WIDTH))],
        out_specs=[row(HEADS_WIDTH), state],
        out_shape=[jax.ShapeDtypeStruct((b, l, HEADS_WIDTH), F32),
                   jax.ShapeDtypeStruct((b, N_HEADS, HEAD_DIM, HEAD_DIM), F32)],
        scratch_shapes=[pltpu.VMEM((N_HEADS, HEAD_DIM, HEAD_DIM), F32),
                        pltpu.VMEM((8 + c, QKV_WIDTH), F32)],
        compiler_params=_params("arbitrary", "arbitrary"),
        name="delta_rule",
    )(ha, hz, hg, conv0, dn0, w_conv, a_log, dt_bias, norm_g)


PAIR_ROWS = 2 * CHUNK
N_PAIRS = N_HEADS // 2


def _attn_kernel(c, tq, mask_before_start, full_f32, q_ref, kp_ref, ks_ref, vp_ref, vs_ref, bias_ref, o_ref,
                 kwin_ref, vwin_ref):
    i = pl.program_id(1)
    prec = HIGHEST if full_f32 else None
    w = BAND_ROWS
    kwin_ref[0:w, :] = kp_ref[...]
    kwin_ref[w:w + tq, :] = ks_ref[...]
    vwin_ref[0:w, :] = vp_ref[...]
    vwin_ref[w:w + tq, :] = vs_ref[...]
    key_idx = lax.broadcasted_iota(jnp.int32, (1, w + c), 1)

    def chunk(j, carry):
        r0 = pl.multiple_of(j * c, c)
        q = q_ref[pl.ds(r0, c), :]
        kk = kwin_ref[pl.ds(r0, w + c), :]
        vv = vwin_ref[pl.ds(r0, w + c), :]
        valid = (i * tq + r0 + key_idx - w) >= 0
        for h in range(N_HEADS):
            sl = slice(h * HEAD_DIM, (h + 1) * HEAD_DIM)
            s = _dot_nt(q[:, sl], kk[:, sl], prec) * HEAD_DIM ** -0.5 + bias_ref[h]
            if mask_before_start:
                s = jnp.where(valid, s, -1e30)
            s = s - jnp.max(s, axis=-1, keepdims=True)
            p = jnp.exp(s)
            p = p / jnp.sum(p, axis=-1, keepdims=True)
            o_ref[pl.ds(r0, c), sl] = _dot(p, vv[:, sl], prec)
        return carry

    lax.fori_loop(0, tq // c, chunk, 0)


def _attention(q, k, v, k_hist, v_hist, bias, full_f32):
    b, l, _ = q.shape
    c = min(CHUNK, l)
    tq = min(ROW_TILE, l)
    assert tq == BAND_ROWS or l == tq
    own = pl.BlockSpec((None, tq, HEADS_WIDTH), lambda bi, i: (bi, i, 0))
    if k_hist is None:
        prev = pl.BlockSpec((None, BAND_ROWS, HEADS_WIDTH), lambda bi, i: (bi, jnp.maximum(i - 1, 0), 0))
        k_prev, v_prev = k, v
    else:
        prev = pl.BlockSpec((None, BAND_ROWS, HEADS_WIDTH), lambda bi, i: (bi, 0, 0))
        k_prev, v_prev = k_hist, v_hist
    return pl.pallas_call(
        functools.partial(_attn_kernel, c, tq, k_hist is None, full_f32),
        grid=(b, l // tq),
        in_specs=[own, prev, own, prev, own,
                  pl.BlockSpec((N_HEADS, c, BAND_ROWS + c), lambda bi, i: (0, 0, 0))],
        out_specs=own,
        out_shape=jax.ShapeDtypeStruct((b, l, HEADS_WIDTH), F32),
        scratch_shapes=[pltpu.VMEM((BAND_ROWS + tq, HEADS_WIDTH), F32),
                        pltpu.VMEM((BAND_ROWS + tq, HEADS_WIDTH), F32)],
        compiler_params=_params("arbitrary", "arbitrary"),
        name="band_attention",
    )(q, k_prev, k, v_prev, v, bias)


STAGED_ROWS = 256
STAGED_CHUNKS = STAGED_ROWS // CHUNK
STAGED_PROBLEMS = STAGED_CHUNKS * N_PAIRS
STAGED_BATCH = 2


def _delta_staged_kernel(qn_ref, kn_ref, v_ref, hz_ref, hg_ref, dn0_ref, alog_ref, dtb_ref, ng_ref,
                         tril_ref, pones_ref,
                         oa_ref, dn_ref,
                         z_ref, gc_ref, beta_ref, gct_ref,
                         pw_ref, sol_ref, a2_ref, ob_ref, qe_ref, kmat_ref, bsw_ref, egl_ref):
    n = pl.program_id(1)
    c, rows = CHUNK, STAGED_ROWS
    lane = lax.broadcasted_iota(jnp.int32, (1, LANES), 1)
    lo = lane < HEAD_DIM
    zero64 = jnp.zeros((HEAD_DIM, HEAD_DIM), F32)

    nb = STAGED_BATCH
    blocks = rows // LANES

    @pl.when(n == 0)
    def _():
        for bb in range(nb):
            for p in range(N_PAIRS):
                top = jnp.concatenate([zero64, dn0_ref[bb, 2 * p + 1]], axis=1)
                bot = jnp.concatenate([dn0_ref[bb, 2 * p], zero64], axis=1)
                z_ref[bb * N_PAIRS + p] = jnp.concatenate([top, bot], axis=0)

    neg_rate = jnp.where(lane < N_HEADS, -jnp.exp(alog_ref[...]), 0.0)
    for bb in range(nb):
        hg = hg_ref[bb]
        gc = sum(_dot(tril_ref[...], piece) for piece in _split3(neg_rate * jax.nn.softplus(hg + dtb_ref[...])))
        gc_ref[bb] = gc
        beta_ref[bb] = jax.nn.sigmoid(hg)
        for blk in range(blocks):
            gct_ref[bb * blocks + blk] = gc[blk * LANES:(blk + 1) * LANES, :].T[0:8, :]

    row_i = lax.broadcasted_iota(jnp.int32, (c, 1), 0)
    col_j = lane % HEAD_DIM
    incl = row_i >= col_j
    strict = row_i > col_j
    pr = lax.broadcasted_iota(jnp.int32, (LANES, LANES), 0) // HEAD_DIM
    pc = lax.broadcasted_iota(jnp.int32, (LANES, LANES), 1) // HEAD_DIM
    anti = pr != pc
    swap = lambda x: jnp.concatenate([x[HEAD_DIM:], x[:HEAD_DIM]], axis=0)
    halves = lambda x: jnp.concatenate([jnp.where(lo, x, jnp.zeros_like(x)), jnp.where(lo, jnp.zeros_like(x), x)],
                                       axis=0)
    problems = [(bb, s, p) for bb in range(nb) for s in range(STAGED_CHUNKS) for p in range(N_PAIRS)]

    def gate_cols(bb, s, p):
        rs = slice(s * c, (s + 1) * c)
        e, o = 2 * p, 2 * p + 1
        gcs = gc_ref[bb, rs, :]
        g_e, g_o = gcs[:, e:e + 1], gcs[:, o:o + 1]
        return g_e, g_o, g_e[c - 1:c, :], g_o[c - 1:c, :]

    for gi, (bb, s, p) in enumerate(problems):
        rs = slice(s * c, (s + 1) * c)
        ls = slice(p * LANES, (p + 1) * LANES)
        e, o = 2 * p, 2 * p + 1
        kp, vp, qp = kn_ref[bb, rs, ls], v_ref[bb, rs, ls], qn_ref[bb, rs, ls]
        ksw = pltpu.roll(kp, HEAD_DIM, axis=1)
        g_e, g_o, _, _ = gate_cols(bb, s, p)
        betas = beta_ref[bb, rs, :]
        b_e, b_o = betas[:, N_HEADS + e:N_HEADS + e + 1], betas[:, N_HEADS + o:N_HEADS + o + 1]
        gt = gct_ref[bb * blocks + s // 2]
        gt_sw = pltpu.roll(gt, HEAD_DIM, axis=1)
        if s % 2 == 0:
            g_row = jnp.where(lo, gt[e:e + 1, :], gt_sw[o:o + 1, :])
        else:
            g_row = jnp.where(lo, gt_sw[e:e + 1, :], gt[o:o + 1, :])
        diff = jnp.where(lo, g_e, g_o) - g_row
        decay = jnp.where(incl, jnp.exp(jnp.where(incl, diff, 0.0)), 0.0)
        kb = kp * jnp.where(lo, b_e, b_o)
        st = _dot_nt(jnp.concatenate([kb, qp], axis=0).astype(BF16), halves(kp.astype(BF16)))
        pw_ref[gi] = jnp.where(strict, -(st[0:c] * decay), 0.0).astype(BF16)
        a2_ref[gi] = jnp.where(incl, st[c:2 * c] * decay, 0.0).astype(BF16)
        sol_ref[gi, :, 0:LANES] = jnp.where(lo, vp, ksw * jnp.exp(g_e)) * b_e
        sol_ref[gi, :, LANES:2 * LANES] = jnp.where(lo, ksw * jnp.exp(g_o), vp) * b_o

    zeros_cl = jnp.zeros((c, LANES), BF16)
    n_stage = int(math.log2(c))
    for stage in range(n_stage):
        tail = stage == n_stage - 1
        for gi in range(len(problems)):
            power = pw_ref[gi]
            sol = sol_ref[gi]
            sb = sol.astype(BF16)
            top = [sb[:, 0:LANES], zeros_cl] + ([] if tail else [jnp.where(lo, power, zeros_cl)])
            bot = [zeros_cl, sb[:, LANES:2 * LANES]] + ([] if tail else [jnp.where(lo, zeros_cl, power)])
            rhs = jnp.concatenate([jnp.concatenate(top, axis=1), jnp.concatenate(bot, axis=1)], axis=0)
            res = _dot(power, rhs)
            sol_ref[gi] = sol + res[:, 0:2 * LANES]
            if not tail:
                pw_ref[gi] = res[:, 2 * LANES:3 * LANES].astype(BF16)

    for gi, (bb, s, p) in enumerate(problems):
        rs = slice(s * c, (s + 1) * c)
        ls = slice(p * LANES, (p + 1) * LANES)
        sol = sol_ref[gi]
        sol_e, sol_o = sol[:, 0:LANES], sol[:, LANES:2 * LANES]
        g_e, g_o, gl_e, gl_o = gate_cols(bb, s, p)
        top = jnp.concatenate([jnp.where(lo, sol_e, 0.0), jnp.where(lo, 0.0, sol_e)], axis=1)
        bot = jnp.concatenate([jnp.where(lo, 0.0, sol_o), jnp.where(lo, sol_o, 0.0)], axis=1)
        y = _dot(a2_ref[gi], jnp.concatenate([top, bot], axis=0).astype(BF16))
        ob_ref[gi] = y[:, 0:LANES]
        qsw = pltpu.roll(qn_ref[bb, rs, ls], HEAD_DIM, axis=1)
        qe_ref[gi] = (qsw * jnp.where(lo, jnp.exp(g_o), jnp.exp(g_e)) - y[:, LANES:2 * LANES]).astype(BF16)
        kt = kn_ref[bb, rs, ls] * jnp.exp(jnp.where(lo, gl_e - g_e, gl_o - g_o))
        kbm = _dot_tn(halves(kt.astype(BF16)), jnp.concatenate([sol_e, sol_o], axis=0).astype(BF16))
        kmat_ref[gi] = jnp.where(anti, kbm, 0.0).astype(BF16)
        bsw_ref[gi] = swap(jnp.where(anti, 0.0, kbm))
        egl_ref[gi] = jnp.exp(jnp.where(lo, gl_e, gl_o))

    for s in range(STAGED_CHUNKS):
        rs = slice(s * c, (s + 1) * c)
        for bb in range(nb):
            for p in range(N_PAIRS):
                gi = (bb * STAGED_CHUNKS + s) * N_PAIRS + p
                zi = bb * N_PAIRS + p
                ls = slice(p * LANES, (p + 1) * LANES)
                zst = z_ref[zi]
                zb = zst.astype(BF16)
                out = ob_ref[gi] + _dot(qe_ref[gi], zb)
                z_ref[zi] = zst * egl_ref[gi] - swap(_dot(kmat_ref[gi], zb)) + bsw_ref[gi]
                ms = _dot((out * out).astype(BF16), pones_ref[...]) * (1.0 / HEAD_DIM)
                oa_ref[bb, rs, ls] = out * lax.rsqrt(ms + RMS_EPS) * ng_ref[:, ls] * _silu(hz_ref[bb, rs, ls])

    @pl.when(n == pl.num_programs(1) - 1)
    def _():
        for bb in range(nb):
            for p in range(N_PAIRS):
                zst = z_ref[bb * N_PAIRS + p]
                dn_ref[bb, 2 * p] = zst[HEAD_DIM:, 0:HEAD_DIM]
                dn_ref[bb, 2 * p + 1] = zst[0:HEAD_DIM, HEAD_DIM:]


def _split3(x):
    a = x.astype(BF16)
    r = x - a.astype(F32)
    b = r.astype(BF16)
    return a, b, (r - b.astype(F32)).astype(BF16)


def _head_ones():
    hi, hj = jnp.arange(HEADS_WIDTH)[:, None] // HEAD_DIM, jnp.arange(HEADS_WIDTH)[None, :] // HEAD_DIM
    return (hi == hj).astype(BF16)


def _delta_staged(qn, kn, v, hz, hg, dn0, a_log, dt_bias, norm_g):
    b, l, _ = qn.shape
    rows, c, nb = STAGED_ROWS, CHUNK, STAGED_BATCH
    g = nb * STAGED_PROBLEMS
    assert b % nb == 0 and l % rows == 0
    ri, ci = jnp.arange(rows)[:, None], jnp.arange(rows)[None, :]
    tril = ((ri >= ci) & (ri // c == ci // c)).astype(BF16)
    pair_ones = _head_ones()[0:LANES, 0:LANES]
    row = lambda w: pl.BlockSpec((nb, rows, w), lambda i, n: (i, n, 0))
    whole = lambda shape: pl.BlockSpec(shape, lambda i, n: (0,) * len(shape))
    state = pl.BlockSpec((nb, N_HEADS, HEAD_DIM, HEAD_DIM), lambda i, n: (i, 0, 0, 0))
    return pl.pallas_call(
        _delta_staged_kernel,
        grid=(b // nb, l // rows),
        in_specs=[row(HEADS_WIDTH), row(HEADS_WIDTH), row(HEADS_WIDTH), row(HEADS_WIDTH), row(GATE_WIDTH), state,
                  whole((1, GATE_WIDTH)), whole((1, GATE_WIDTH)),
                  whole((1, HEADS_WIDTH)), whole((rows, rows)), whole((LANES, LANES))],
        out_specs=[row(HEADS_WIDTH), state],
        out_shape=[jax.ShapeDtypeStruct((b, l, HEADS_WIDTH), F32),
                   jax.ShapeDtypeStruct((b, N_HEADS, HEAD_DIM, HEAD_DIM), F32)],
        scratch_shapes=[pltpu.VMEM((nb * N_PAIRS, LANES, LANES), F32),
                        pltpu.VMEM((nb, rows, GATE_WIDTH), F32),
                        pltpu.VMEM((nb, rows, GATE_WIDTH), F32),
                        pltpu.VMEM((nb * rows // LANES, 8, LANES), F32),
                        pltpu.VMEM((g, c, LANES), BF16),
                        pltpu.VMEM((g, c, 2 * LANES), F32),
                        pltpu.VMEM((g, c, LANES), BF16),
                        pltpu.VMEM((g, c, LANES), F32),
                        pltpu.VMEM((g, c, LANES), BF16),
                        pltpu.VMEM((g, LANES, LANES), BF16),
                        pltpu.VMEM((g, LANES, LANES), F32),
                        pltpu.VMEM((g, 1, LANES), F32)],
        compiler_params=_params("arbitrary", "arbitrary"),
        name="delta_rule_staged",
    )(qn, kn, v, hz, hg, dn0, a_log, dt_bias, norm_g, tril, pair_ones)


ATTN_WINDOW = BAND_ROWS + PAIR_ROWS


def _attn_pairs_kernel(q_ref, kp_ref, ks_ref, vp_ref, vs_ref, bias_ref, o_ref, kwin_ref, vwin_ref):
    i = pl.program_id(1)
    w, tq = BAND_ROWS, ROW_TILE
    kwin_ref[0:w, :] = kp_ref[...]
    kwin_ref[w:w + tq, :] = ks_ref[...]
    vwin_ref[0:w, :] = vp_ref[...]
    vwin_ref[w:w + tq, :] = vs_ref[...]
    lo = lax.broadcasted_iota(jnp.int32, (1, LANES), 1) < HEAD_DIM
    key_idx = lax.broadcasted_iota(jnp.int32, (1, ATTN_WINDOW), 1)
    ones = jnp.ones((ATTN_WINDOW, LANES), BF16)

    def chunk_pair(mask_start, m, carry):
        r0 = pl.multiple_of(m * PAIR_ROWS, PAIR_ROWS)
        for p in range(N_PAIRS):
            ls = slice(p * LANES, (p + 1) * LANES)
            q = q_ref[pl.ds(r0, PAIR_ROWS), ls] * HEAD_DIM ** -0.5
            kk = kwin_ref[pl.ds(r0, ATTN_WINDOW), ls]
            vv = vwin_ref[pl.ds(r0, ATTN_WINDOW), ls]
            zero = jnp.zeros_like(q)
            q2 = jnp.concatenate([jnp.where(lo, q, zero), jnp.where(lo, zero, q)], axis=0)
            s = _dot_nt(q2, kk) + bias_ref[p]
            if mask_start:
                s = jnp.where(r0 + key_idx < w, -1e30, s)
            pexp = jnp.exp(s - jnp.max(s, axis=-1, keepdims=True)).astype(BF16)
            pv = _dot(pexp, jnp.concatenate([vv, ones], axis=1))
            pv = pv[:, 0:LANES] / pv[:, LANES:2 * LANES]
            o_ref[pl.ds(r0, PAIR_ROWS), ls] = jnp.where(lo, pv[0:PAIR_ROWS], pv[PAIR_ROWS:2 * PAIR_ROWS])
        return carry

    @pl.when(i == 0)
    def _():
        lax.fori_loop(0, tq // PAIR_ROWS, functools.partial(chunk_pair, True), 0, unroll=4)

    @pl.when(i > 0)
    def _():
        lax.fori_loop(0, tq // PAIR_ROWS, functools.partial(chunk_pair, False), 0, unroll=4)


def _pair_bias(rel_table):
    a = jnp.arange(PAIR_ROWS)[:, None]
    idx = jnp.arange(ATTN_WINDOW)[None, :]
    first = (a // CHUNK) * CHUNK
    in_band = (idx >= first) & (idx < first + BAND_ROWS + CHUNK)
    bias = jnp.where(in_band[None], _rel_bias(rel_table, PAIR_ROWS), -1e30)
    return bias.reshape(N_PAIRS, 2 * PAIR_ROWS, ATTN_WINDOW)


def _attention_pairs(qkv, bias):
    b, l, _ = qkv.shape
    tq = ROW_TILE
    assert tq == BAND_ROWS and l % tq == 0
    own = lambda j: pl.BlockSpec((None, tq, HEADS_WIDTH), lambda bi, i: (bi, i, j))
    prev = lambda j: pl.BlockSpec((None, BAND_ROWS, HEADS_WIDTH), lambda bi, i: (bi, jnp.maximum(i - 1, 0), j))
    return pl.pallas_call(
        _attn_pairs_kernel,
        grid=(b, l // tq),
        in_specs=[own(0), prev(1), own(1), prev(2), own(2),
                  pl.BlockSpec((N_PAIRS, 2 * PAIR_ROWS, ATTN_WINDOW), lambda bi, i: (0, 0, 0))],
        out_specs=own(0),
        out_shape=jax.ShapeDtypeStruct((b, l, HEADS_WIDTH), F32),
        scratch_shapes=[pltpu.VMEM((BAND_ROWS + tq, HEADS_WIDTH), BF16),
                        pltpu.VMEM((BAND_ROWS + tq, HEADS_WIDTH), BF16)],
        compiler_params=_params("arbitrary", "arbitrary"),
        name="band_attention_pairs",
    )(qkv, qkv, qkv, qkv, qkv, bias)


def _mix_kernel(tile, start, full_f32, oa_ref, ob_ref, hc_ref, x_ref, pool0_ref, pw_ref, ps_ref, wout_ref, g_ref, b_ref,
                x1_ref, ext_ref):
    i = pl.program_id(1)
    prec = HIGHEST if full_f32 else None
    wdt = wout_ref.dtype
    hist0 = 16 - POOL_HIST

    @pl.when(i == 0)
    def _():
        ext_ref[0:hist0, :] = jnp.zeros((hist0, POOL_WIDTH), F32)
        ext_ref[hist0:16, :] = pool0_ref[...]

    u = hc_ref[...]
    ext_ref[16:16 + tile, :] = u
    window = 2 << (lax.broadcasted_iota(jnp.int32, (1, POOL_WIDTH), 1) // POOL_GROUP_DIM)
    level = ext_ref[...]
    sums = {}
    for wdw in POOL_WINDOWS:
        level = level + pltpu.roll(level, wdw // 2, axis=0)
        sums[wdw] = level[16:16 + tile, :]
    wsum = sums[POOL_WINDOWS[-1]]
    for wdw in reversed(POOL_WINDOWS[:-1]):
        wsum = jnp.where(window == wdw, sums[wdw], wsum)
    ext_ref[hist0:16, :] = u[tile - POOL_HIST:tile, :]
    pos1 = start + i * tile + 1 + lax.broadcasted_iota(jnp.int32, (tile, 1), 0)
    cnt = jnp.minimum(pos1, window).astype(F32)
    res = wsum / cnt - u
    oc = _dot(res, pw_ref[...], prec) * ps_ref[...]

    mixed = jnp.concatenate([oa_ref[...].astype(wdt), ob_ref[...].astype(wdt), oc.astype(wdt)], axis=1)
    mix = _dot(mixed, wout_ref[...], prec)
    x1_ref[...] = _layer_norm(DEEPNORM_ALPHA * x_ref[...] + mix, g_ref[...], b_ref[...])


def _mix(oa, ob, hc, x, pool0, pool_wbd, pool_scale, w_out, ln_g, ln_b, start, full_f32):
    b, l, _ = x.shape
    tile = min(ROW_TILE, l)
    row = lambda w: pl.BlockSpec((None, tile, w), lambda bi, i: (bi, i, 0))
    whole = lambda shape: pl.BlockSpec(shape, lambda bi, i: (0,) * len(shape))
    return pl.pallas_call(
        functools.partial(_mix_kernel, tile, start, full_f32),
        grid=(b, l // tile),
        in_specs=[row(HEADS_WIDTH), row(HEADS_WIDTH), row(POOL_WIDTH), row(D_MODEL),
                  pl.BlockSpec((None, POOL_HIST, POOL_WIDTH), lambda bi, i: (bi, 0, 0)),
                  whole((POOL_WIDTH, POOL_WIDTH)), whole((1, POOL_WIDTH)), whole((D_MODEL, D_MODEL)),
                  whole((1, D_MODEL)), whole((1, D_MODEL))],
        out_specs=row(D_MODEL),
        out_shape=jax.ShapeDtypeStruct((b, l, D_MODEL), F32),
        scratch_shapes=[pltpu.VMEM((16 + tile, POOL_WIDTH), F32)],
        compiler_params=_params("arbitrary", "arbitrary"),
        name="pool_outproj_ln1",
    )(oa, ob, hc, x, pool0, pool_wbd, pool_scale, w_out, ln_g, ln_b)


def _top2_of4(a, b, c, d):
    hi1, lo1 = jnp.maximum(a, b), jnp.minimum(a, b)
    hi2, lo2 = jnp.maximum(c, d), jnp.minimum(c, d)
    return jnp.maximum(hi1, hi2), jnp.maximum(jnp.minimum(hi1, hi2), jnp.maximum(lo1, lo2))


def _route(logits_t, bias_t):
    aff = jax.nn.sigmoid(logits_t[0:N_EXPERTS, :])
    sel = aff + bias_t[0:N_EXPERTS, :]
    t = sel.shape[1]
    scores = []
    for gi in range(N_EXPERT_GROUPS):
        r = [sel[4 * gi + m:4 * gi + m + 1, :] for m in range(EXPERTS_PER_GROUP)]
        top1, top2 = _top2_of4(*r)
        scores.append(top1 + top2)
    best = scores[0]
    best_g = jnp.zeros((1, t), jnp.int32)
    for gi in range(1, N_EXPERT_GROUPS):
        better = scores[gi] > best
        best = jnp.where(better, scores[gi], best)
        best_g = jnp.where(better, gi, best_g)
    e_idx = lax.broadcasted_iota(jnp.int32, (N_EXPERTS, t), 0)
    masked = jnp.where(e_idx // EXPERTS_PER_GROUP == best_g, sel, -jnp.inf)
    m1 = jnp.max(masked, axis=0, keepdims=True)
    i1 = jnp.min(jnp.where(masked == m1, e_idx, N_EXPERTS), axis=0, keepdims=True)
    rest = jnp.where(e_idx == i1, -jnp.inf, masked)
    m2 = jnp.max(rest, axis=0, keepdims=True)
    i2 = jnp.min(jnp.where(rest == m2, e_idx, N_EXPERTS), axis=0, keepdims=True)
    w1 = jnp.sum(jnp.where(e_idx == i1, aff, 0.0), axis=0, keepdims=True)
    w2 = jnp.sum(jnp.where(e_idx == i2, aff, 0.0), axis=0, keepdims=True)
    tot = w1 + w2
    gates = jnp.where(e_idx == i1, w1 / tot, 0.0) + jnp.where(e_idx == i2, w2 / tot, 0.0)
    return jnp.concatenate([gates, jnp.zeros((LANES - N_EXPERTS, t), F32)], axis=0), best_g


def _expert(xb, w_gu, w_d, gates, e, prec):
    gu = _dot(xb, w_gu, prec)
    hid = _silu(gu[:, 0:D_EXPERT]) * gu[:, D_EXPERT:2 * D_EXPERT]
    lane = lax.broadcasted_iota(jnp.int32, (1, LANES), 1)
    ge = jnp.sum(jnp.where(lane == e, gates, 0.0), axis=-1, keepdims=True)
    return _dot(hid.astype(w_d.dtype), w_d, prec) * ge


def _ffn_kernel(x_ref, wr_ref, rb_ref, wgu_ref, wd_ref, g_ref, b_ref, y_ref):
    x = x_ref[...]
    xb = x.astype(BF16)
    logits = _dot(xb, wr_ref[...])
    gates = _route(logits.T, rb_ref[...])[0].T
    acc = jnp.zeros(x.shape, F32)
    for e in range(N_EXPERTS):
        acc = acc + _expert(xb, wgu_ref[e], wd_ref[e], gates, e, None)
    y_ref[...] = _layer_norm(DEEPNORM_ALPHA * x + acc, g_ref[...], b_ref[...])


def _ffn_f32_kernel(x_ref, wr_ref, rb_ref, wgu_ref, wd_ref, g_ref, b_ref, y_ref, gates_ref, acc_ref):
    e = pl.program_id(1)
    x = x_ref[...]

    @pl.when(e == 0)
    def _():
        logits = _dot(x, wr_ref[...], HIGHEST)
        gates_ref[...] = _route(logits.T, rb_ref[...])[0].T
        acc_ref[...] = jnp.zeros(acc_ref.shape, F32)

    acc_ref[...] += _expert(x, wgu_ref[...], wd_ref[...], gates_ref[...], e, HIGHEST)

    @pl.when(e == N_EXPERTS - 1)
    def _():
        y_ref[...] = _layer_norm(DEEPNORM_ALPHA * x + acc_ref[...], g_ref[...], b_ref[...])


def _ffn(x2d, w_router, router_bias, w_gu, w_down, ln_g, ln_b, full_f32):
    t = x2d.shape[0]
    tm = min(ROW_TILE, t)
    out_shape = jax.ShapeDtypeStruct((t, D_MODEL), F32)
    if full_f32:
        whole = lambda shape: pl.BlockSpec(shape, lambda i, e: (0,) * len(shape))
        return pl.pallas_call(
            _ffn_f32_kernel,
            grid=(t // tm, N_EXPERTS),
            in_specs=[pl.BlockSpec((tm, D_MODEL), lambda i, e: (i, 0)),
                      whole((D_MODEL, LANES)), whole((LANES, 1)),
                      pl.BlockSpec((None, D_MODEL, 2 * D_EXPERT), lambda i, e: (e, 0, 0)),
                      pl.BlockSpec((None, D_EXPERT, D_MODEL), lambda i, e: (e, 0, 0)),
                      whole((1, D_MODEL)), whole((1, D_MODEL))],
            out_specs=pl.BlockSpec((tm, D_MODEL), lambda i, e: (i, 0)),
            out_shape=out_shape,
            scratch_shapes=[pltpu.VMEM((tm, LANES), F32), pltpu.VMEM((tm, D_MODEL), F32)],
            compiler_params=_params("arbitrary", "arbitrary"),
            name="routed_ffn_ln2_f32",
        )(x2d, w_router, router_bias, w_gu, w_down, ln_g, ln_b)
    whole = lambda shape: pl.BlockSpec(shape, lambda i: (0,) * len(shape), pipeline_mode=pl.Buffered(1))
    return pl.pallas_call(
        _ffn_kernel,
        grid=(t // tm,),
        in_specs=[pl.BlockSpec((tm, D_MODEL), lambda i: (i, 0)),
                  whole((D_MODEL, LANES)), whole((LANES, 1)),
                  whole((N_EXPERTS, D_MODEL, 2 * D_EXPERT)), whole((N_EXPERTS, D_EXPERT, D_MODEL)),
                  whole((1, D_MODEL)), whole((1, D_MODEL))],
        out_specs=pl.BlockSpec((tm, D_MODEL), lambda i: (i, 0)),
        out_shape=out_shape,
        compiler_params=_params("arbitrary"),
        name="routed_ffn_ln2",
    )(x2d, w_router, router_bias, w_gu, w_down, ln_g, ln_b)


def _prep_layer(l, wdt, w_in, w_conv, a_log, dt_bias, dn_norm_g, rel_table, pool_w, pool_scale, w_out,
                ln1_g, ln1_b, w_gate, w_up, w_down, ln2_g, ln2_b):
    wi = w_in[l]
    gates = jnp.zeros((D_MODEL, GATE_WIDTH), F32).at[:, 0:2 * N_HEADS].set(wi[:, OFF_AA:OFF_BQ])
    w_perm = jnp.concatenate([wi[:, OFF_AQ:OFF_AZ], wi[:, OFF_AZ:OFF_AA], wi[:, OFF_BQ:OFF_CU],
                              wi[:, OFF_CU:IN_WIDTH], gates], axis=1).astype(wdt)
    pad_heads = lambda vec: jnp.zeros((1, GATE_WIDTH), F32).at[0, 0:N_HEADS].set(vec)
    pool_wbd = jnp.zeros((POOL_WIDTH, POOL_WIDTH), F32)
    for gi in range(len(POOL_WINDOWS)):
        s = slice(gi * POOL_GROUP_DIM, (gi + 1) * POOL_GROUP_DIM)
        pool_wbd = pool_wbd.at[s, s].set(pool_w[l, gi])
    return dict(
        w_perm=w_perm, w_conv=w_conv[l], a_log=pad_heads(a_log[l]), dt_bias=pad_heads(dt_bias[l]),
        norm_g=jnp.tile(dn_norm_g[l], N_HEADS)[None, :], rel_table=rel_table[l],
        pool_wbd=pool_wbd, pool_scale=pool_scale[l][None, :], w_out=w_out[l].astype(wdt),
        ln1_g=ln1_g[l][None, :], ln1_b=ln1_b[l][None, :],
        w_gu=jnp.concatenate([w_gate[l], w_up[l]], axis=-1).astype(wdt), w_down=w_down[l].astype(wdt),
        ln2_g=ln2_g[l][None, :], ln2_b=ln2_b[l][None, :])


def _rel_bias(rel_table, c):
    w = BAND_ROWS + c
    period = c + w
    m = jnp.arange(period)
    delta = jnp.where(m < w, -m, period - m)
    vec = rel_table[:, jnp.clip(delta + BAND_ROWS, -REL_CLIP, REL_CLIP) + REL_CLIP].astype(F32)
    rolled = jnp.tile(vec, (1, c))[:, :c * (period - 1)].reshape(rel_table.shape[0], c, period - 1)
    return rolled[:, :, :w]


def _last_rows(t, n):
    if t.shape[1] < n:
        t = jnp.concatenate([jnp.zeros((t.shape[0], n - t.shape[1]) + t.shape[2:], t.dtype), t], axis=1)
    return t[:, t.shape[1] - n:]


def _trunk(x, start, dn_state, conv_state, k_hist, v_hist, pool_state, band_rows, layers, w_router, router_bias,
           full_f32):
    b, l, _ = x.shape
    assert l >= POOL_HIST
    c = min(CHUNK, l)
    new_dn, new_conv, new_k, new_v, new_pool = [], [], [], [], []
    for li, p in enumerate(layers):
        ha, hz, hb, hc, hg = _inproj(x.reshape(b * l, D_MODEL), p["w_perm"], full_f32)
        ha = ha.reshape(b, l, QKV_WIDTH)
        hz = hz.reshape(b, l, HEADS_WIDTH)
        hb = hb.reshape(b, l, QKV_WIDTH)
        hc = hc.reshape(b, l, POOL_WIDTH)
        hg = hg.reshape(b, l, GATE_WIDTH)
        oa, dn_new = _delta(ha, hz, hg, conv_state[li], dn_state[li], p["w_conv"], p["a_log"], p["dt_bias"],
                            p["norm_g"], full_f32)
        qb, kb, vb = (hb[..., j * HEADS_WIDTH:(j + 1) * HEADS_WIDTH] for j in range(3))
        bias = _rel_bias(p["rel_table"], c)
        if k_hist is None:
            ob = _attention(qb, kb, vb, None, None, bias, full_f32)
            k_new, v_new = _last_rows(kb, band_rows), _last_rows(vb, band_rows)
        else:
            kh = k_hist[li].reshape(b, -1, HEADS_WIDTH)
            vh = v_hist[li].reshape(b, -1, HEADS_WIDTH)
            ob = _attention(qb, kb, vb, kh, vh, bias, full_f32)
            k_new = _last_rows(jnp.concatenate([kh, kb], axis=1), band_rows)
            v_new = _last_rows(jnp.concatenate([vh, vb], axis=1), band_rows)
        x1 = _mix(oa, ob, hc, x, pool_state[li], p["pool_wbd"], p["pool_scale"], p["w_out"], p["ln1_g"],
                  p["ln1_b"], start, full_f32)
        x = _ffn(x1.reshape(b * l, D_MODEL), w_router.astype(p["w_out"].dtype), router_bias, p["w_gu"],
                 p["w_down"], p["ln2_g"], p["ln2_b"], full_f32).reshape(b, l, D_MODEL)
        new_dn.append(dn_new)
        new_conv.append(ha[:, l - (DN_CONV - 1):])
        new_k.append(k_new.reshape(b, band_rows, N_HEADS, HEAD_DIM))
        new_v.append(v_new.reshape(b, band_rows, N_HEADS, HEAD_DIM))
        new_pool.append(hc[:, l - POOL_HIST:])
    return (x, jnp.stack(new_dn), jnp.stack(new_conv), jnp.stack(new_k), jnp.stack(new_v), jnp.stack(new_pool))


def _trunk_seq(x, dn_state, conv_state, pool_state, band_rows, layers, w_router, router_bias):
    b, l, _ = x.shape
    assert band_rows == BAND_ROWS and l >= band_rows
    new_dn, new_conv, new_k, new_v, new_pool = [], [], [], [], []
    for li, p in enumerate(layers):
        qn, kn, va, hz, hb, hc, hg, kv_last, conv_new = _inproj_seq(x, p["w_perm"], conv_state[li], p["w_conv"])
        oa, dn_new = _delta_staged(qn, kn, va, hz, hg, dn_state[li], p["a_log"], p["dt_bias"], p["norm_g"])
        ob = _attention_pairs(hb, _pair_bias(p["rel_table"]))
        x1 = _mix(oa, ob, hc, x, pool_state[li], p["pool_wbd"], p["pool_scale"], p["w_out"], p["ln1_g"],
                  p["ln1_b"], 0, False)
        x = _ffn(x1.reshape(b * l, D_MODEL), w_router.astype(BF16), router_bias, p["w_gu"], p["w_down"],
                 p["ln2_g"], p["ln2_b"], False).reshape(b, l, D_MODEL)
        new_dn.append(dn_new)
        new_conv.append(conv_new)
        new_k.append(kv_last[..., 0:HEADS_WIDTH].reshape(b, band_rows, N_HEADS, HEAD_DIM))
        new_v.append(kv_last[..., HEADS_WIDTH:].reshape(b, band_rows, N_HEADS, HEAD_DIM))
        new_pool.append(hc[:, l - POOL_HIST:])
    return (x, jnp.stack(new_dn), jnp.stack(new_conv), jnp.stack(new_k), jnp.stack(new_v), jnp.stack(new_pool))


@jax.jit
def kernel(x_prompt, x_sample, state_dn, state_conv, cache_k, cache_v, state_pool, w_in, w_conv, a_log, dt_bias,
           dn_norm_g, rel_table, pool_w, pool_scale, w_out, ln1_g, ln1_b, w_router, router_bias, w_gate, w_up,
           w_down, ln2_g, ln2_b):
    weights = (w_in, w_conv, a_log, dt_bias, dn_norm_g, rel_table, pool_w, pool_scale, w_out,
               ln1_g, ln1_b, w_gate, w_up, w_down, ln2_g, ln2_b)
    layers_bf16 = [_prep_layer(l, BF16, *weights) for l in range(DEPTH)]
    layers_f32 = [_prep_layer(l, F32, *weights) for l in range(DEPTH)]
    w_router_p = jnp.zeros((D_MODEL, LANES), F32).at[:, 0:N_EXPERTS].set(w_router)
    router_bias_p = jnp.zeros((LANES, 1), F32).at[0:N_EXPERTS, 0].set(router_bias)
    band_rows = cache_k.shape[2]
    bp = x_prompt.shape[0]
    zero_dn = jnp.zeros((DEPTH, bp, N_HEADS, HEAD_DIM, HEAD_DIM), state_dn.dtype)
    zero_conv = jnp.zeros((DEPTH, bp, DN_CONV - 1, QKV_WIDTH), x_prompt.dtype)
    zero_pool = jnp.zeros((DEPTH, bp, POOL_HIST, POOL_WIDTH), x_prompt.dtype)
    prompt = _trunk_seq(x_prompt, zero_dn, zero_conv, zero_pool, band_rows, layers_bf16, w_router_p, router_bias_p)
    sample = _trunk(x_sample, PAST_LEN, state_dn, state_conv, cache_k, cache_v, state_pool, band_rows, layers_f32,
                    w_router_p, router_bias_p, True)
    return (prompt[0], sample[0]) + prompt[1:] + sample[1:]
```

```python
import functools
import math

import jax
import jax.numpy as jnp
from jax import lax
from jax.experimental import pallas as pl
from jax.experimental.pallas import tpu as pltpu

F32 = jnp.float32
BF16 = jnp.bfloat16
HIGHEST = lax.Precision.HIGHEST

D_MODEL = 1024
HEAD_DIM = 64
N_HEADS = 6
HEADS_WIDTH = N_HEADS * HEAD_DIM
QKV_WIDTH = 3 * HEADS_WIDTH
DN_CONV = 4
CHUNK = 64
BAND_ROWS = 512
REL_CLIP = 128
POOL_WINDOWS = (2, 4, 8, 16)
POOL_GROUP_DIM = 64
POOL_WIDTH = 256
POOL_HIST = 15
N_EXPERTS = 16
N_EXPERT_GROUPS = 4
EXPERTS_PER_GROUP = 4
D_EXPERT = 256
DEPTH = 2
DEEPNORM_ALPHA = (2 * DEPTH) ** 0.25
LN_EPS = 1e-5
RMS_EPS = 1e-6
PAST_LEN = 1024

OFF_AQ = 0
OFF_AZ = 3 * HEADS_WIDTH
OFF_AA = OFF_AZ + HEADS_WIDTH
OFF_AB = OFF_AA + N_HEADS
OFF_BQ = OFF_AB + N_HEADS
OFF_CU = OFF_BQ + 3 * HEADS_WIDTH
IN_WIDTH = OFF_CU + POOL_WIDTH

LANES = 128
GATE_WIDTH = LANES
SEG_A = (0, QKV_WIDTH)
SEG_Z = (SEG_A[1], SEG_A[1] + HEADS_WIDTH)
SEG_B = (SEG_Z[1], SEG_Z[1] + QKV_WIDTH)
SEG_C = (SEG_B[1], SEG_B[1] + POOL_WIDTH)
SEG_G = (SEG_C[1], SEG_C[1] + GATE_WIDTH)
PERM_WIDTH = SEG_G[1]

ROW_TILE = 512
VMEM_LIMIT = 56 * 1024 * 1024


def _params(*sem):
    return pltpu.CompilerParams(dimension_semantics=sem, vmem_limit_bytes=VMEM_LIMIT)


def _dot(a, b, precision=None):
    return jnp.dot(a, b, preferred_element_type=F32, precision=precision)


def _dot_nt(a, b, precision=None):
    return lax.dot_general(a, b, (((1,), (1,)), ((), ())), preferred_element_type=F32, precision=precision)


def _dot_tn(a, b, precision=None):
    return lax.dot_general(a, b, (((0,), (0,)), ((), ())), preferred_element_type=F32, precision=precision)


def _silu(x):
    return x * jax.nn.sigmoid(x)


def _layer_norm(x, g, b):
    mu = jnp.mean(x, axis=-1, keepdims=True)
    xc = x - mu
    var = jnp.mean(xc * xc, axis=-1, keepdims=True)
    return xc * lax.rsqrt(var + LN_EPS) * g + b


def _inproj_kernel(full_f32, x_ref, w_ref, ha_ref, hz_ref, hb_ref, hc_ref, hg_ref):
    prec = HIGHEST if full_f32 else None
    xb = x_ref[...].astype(w_ref.dtype)
    for (lo, hi), out_ref in ((SEG_A, ha_ref), (SEG_Z, hz_ref), (SEG_B, hb_ref), (SEG_C, hc_ref), (SEG_G, hg_ref)):
        step = 384 if (hi - lo) % 384 == 0 else hi - lo
        for c0 in range(0, hi - lo, step):
            out_ref[:, c0:c0 + step] = _dot(xb, w_ref[:, lo + c0:lo + c0 + step], prec)


def _inproj(x2d, w_perm, full_f32):
    t = x2d.shape[0]
    tm = min(ROW_TILE, t)
    widths = [s[1] - s[0] for s in (SEG_A, SEG_Z, SEG_B, SEG_C, SEG_G)]
    return pl.pallas_call(
        functools.partial(_inproj_kernel, full_f32),
        grid=(t // tm,),
        in_specs=[pl.BlockSpec((tm, D_MODEL), lambda i: (i, 0)),
                  pl.BlockSpec((D_MODEL, PERM_WIDTH), lambda i: (0, 0))],
        out_specs=[pl.BlockSpec((tm, w), lambda i: (i, 0)) for w in widths],
        out_shape=[jax.ShapeDtypeStruct((t, w), F32) for w in widths],
        compiler_params=_params("arbitrary"),
        name="inproj",
    )(x2d, w_perm)


INPROJ_COLS = 512


def _inproj_seq_kernel(x_ref, w_ref, conv0_ref, wconv_ref, hones_ref,
                       qn_ref, kn_ref, v_ref, hz_ref, hb_ref, hc_ref, hg_ref, kv_ref, cs_ref, ext_ref):
    i = pl.program_id(1)
    tm = x_ref.shape[0]
    pad = 8 - (DN_CONV - 1)

    @pl.when(i == 0)
    def _():
        ext_ref[pad:8, :] = conv0_ref[...]

    xb = x_ref[...].astype(BF16)
    segments = ((SEG_Z, hz_ref, 0), (SEG_B, hb_ref, 0), (SEG_C, hc_ref, 0), (SEG_G, hg_ref, 0), (SEG_A, ext_ref, 8))
    kv0 = SEG_B[0] + HEADS_WIDTH
    for c0 in range(0, PERM_WIDTH, INPROJ_COLS):
        h = _dot(xb, w_ref[:, c0:c0 + INPROJ_COLS])
        for (lo, hi), out_ref, row0 in segments:
            a, b = max(lo, c0), min(hi, c0 + INPROJ_COLS)
            if a < b:
                out_ref[row0:row0 + tm, a - lo:b - lo] = h[:, a - c0:b - c0].astype(out_ref.dtype)
        a, b = max(kv0, c0), min(SEG_B[1], c0 + INPROJ_COLS)
        if a < b:
            kv_ref[:, a - kv0:b - kv0] = h[:, a - c0:b - c0]

    for r0 in range(0, tm, LANES):
        for part, dst in enumerate((qn_ref, kn_ref, v_ref)):
            cs = slice(part * HEADS_WIDTH, (part + 1) * HEADS_WIDTH)
            conv = ext_ref[pad + r0:pad + r0 + LANES, cs] * wconv_ref[0:1, cs]
            for j in range(1, DN_CONV):
                conv = conv + ext_ref[pad + j + r0:pad + j + r0 + LANES, cs] * wconv_ref[j:j + 1, cs]
            act = _silu(conv)
            if part < 2:
                inv = lax.rsqrt(_dot((act * act).astype(BF16), hones_ref[...]) + RMS_EPS)
                act = act * (inv * HEAD_DIM ** -0.5 if part == 0 else inv)
            dst[r0:r0 + LANES, :] = act
    last = ext_ref[8 + tm - (DN_CONV - 1):8 + tm, :]
    cs_ref[...] = last
    ext_ref[pad:8, :] = last


def _inproj_seq(x, w_perm, conv0, w_conv):
    b, l, _ = x.shape
    tm = ROW_TILE
    assert tm == BAND_ROWS and l % tm == 0
    widths = [HEADS_WIDTH] * 4 + [s[1] - s[0] for s in (SEG_B, SEG_C, SEG_G)]
    dtypes = [F32, F32, F32, F32, BF16, F32, F32]
    row = lambda w: pl.BlockSpec((None, tm, w), lambda bi, i: (bi, i, 0))
    whole = lambda shape: pl.BlockSpec(shape, lambda bi, i: (0,) * len(shape))
    conv_rows = pl.BlockSpec((None, DN_CONV - 1, QKV_WIDTH), lambda bi, i: (bi, 0, 0))
    return pl.pallas_call(
        _inproj_seq_kernel,
        grid=(b, l // tm),
        in_specs=[row(D_MODEL), whole((D_MODEL, PERM_WIDTH)), conv_rows, whole((DN_CONV, QKV_WIDTH)),
                  whole((HEADS_WIDTH, HEADS_WIDTH))],
        out_specs=[row(w) for w in widths]
        + [pl.BlockSpec((None, tm, 2 * HEADS_WIDTH), lambda bi, i: (bi, 0, 0)), conv_rows],
        out_shape=[jax.ShapeDtypeStruct((b, l, w), d) for w, d in zip(widths, dtypes)]
        + [jax.ShapeDtypeStruct((b, tm, 2 * HEADS_WIDTH), F32),
           jax.ShapeDtypeStruct((b, DN_CONV - 1, QKV_WIDTH), F32)],
        scratch_shapes=[pltpu.VMEM((8 + tm, QKV_WIDTH), F32)],
        compiler_params=_params("arbitrary", "arbitrary"),
        name="inproj_seq",
    )(x, w_perm, conv0, w_conv, _head_ones())


def _delta_kernel(c, full_f32, ha_ref, hz_ref, hg_ref, conv0_ref, dn0_ref, wconv_ref, alog_ref, dtb_ref, ng_ref,
                  oa_ref, dn_ref, s_ref, ext_ref):
    n = pl.program_id(1)
    prec = HIGHEST if full_f32 else None
    pad = 8 - (DN_CONV - 1)

    @pl.when(n == 0)
    def _():
        s_ref[...] = dn0_ref[...]
        ext_ref[pad:8, :] = conv0_ref[...]

    u = ha_ref[...]
    ext_ref[8:8 + c, :] = u
    conv = ext_ref[pad:pad + c, :] * wconv_ref[0:1, :]
    for j in range(1, DN_CONV):
        conv = conv + ext_ref[pad + j:pad + j + c, :] * wconv_ref[j:j + 1, :]
    ext_ref[pad:8, :] = u[c - (DN_CONV - 1):c, :]
    qkv = _silu(conv)
    q = qkv[:, 0:HEADS_WIDTH]
    k = qkv[:, HEADS_WIDTH:2 * HEADS_WIDTH]
    v = qkv[:, 2 * HEADS_WIDTH:QKV_WIDTH]

    r = lax.broadcasted_iota(jnp.int32, (HEADS_WIDTH, HEADS_WIDTH), 0) // HEAD_DIM
    cc = lax.broadcasted_iota(jnp.int32, (HEADS_WIDTH, HEADS_WIDTH), 1) // HEAD_DIM
    head_ones = (r == cc).astype(F32)
    qn = q * lax.rsqrt(_dot(q * q, head_ones, HIGHEST) + RMS_EPS) * HEAD_DIM ** -0.5
    kn = k * lax.rsqrt(_dot(k * k, head_ones, HIGHEST) + RMS_EPS)

    hg = hg_ref[...]
    lane = lax.broadcasted_iota(jnp.int32, (1, GATE_WIDTH), 1)
    neg_rate = jnp.where(lane < N_HEADS, -jnp.exp(alog_ref[...]), 0.0)
    g = neg_rate * jax.nn.softplus(hg + dtb_ref[...])
    beta = jax.nn.sigmoid(hg)
    ri = lax.broadcasted_iota(jnp.int32, (c, c), 0)
    ci = lax.broadcasted_iota(jnp.int32, (c, c), 1)
    incl = ri >= ci
    strict = ri > ci
    gc = _dot(incl.astype(F32), g, HIGHEST)
    gc_t = _dot_tn(g, (ri <= ci).astype(F32), HIGHEST)

    z = hz_ref[...]
    for h in range(N_HEADS):
        sl = slice(h * HEAD_DIM, (h + 1) * HEAD_DIM)
        qh, kh, vh = qn[:, sl], kn[:, sl], v[:, sl]
        gch = gc[:, h:h + 1]
        diff = gch - gc_t[h:h + 1, :]
        decay = jnp.where(incl, jnp.exp(jnp.where(incl, diff, 0.0)), 0.0)
        bh = beta[:, N_HEADS + h:N_HEADS + h + 1]
        kbeta = kh * bh
        eg = jnp.exp(gch)
        neg_m = jnp.where(strict, -(_dot_nt(kbeta, kh, prec) * decay), 0.0)
        sol = jnp.concatenate([vh * bh, kbeta * eg], axis=-1)
        power = neg_m
        sol = sol + _dot(power, sol, prec)
        for _ in range(int(math.log2(c)) - 1):
            power = _dot(power, power, prec)
            sol = sol + _dot(power, sol, prec)
        u_h, w_h = sol[:, 0:HEAD_DIM], sol[:, HEAD_DIM:2 * HEAD_DIM]
        a_intra = jnp.where(incl, _dot_nt(qh, kh, prec) * decay, 0.0)
        g_last = gc[c - 1:c, h:h + 1]
        k_tail = kh * jnp.exp(g_last - gch)
        s = s_ref[h]
        v_new = u_h - _dot(w_h, s, prec)
        o = _dot(qh * eg, s, prec) + _dot(a_intra, v_new, prec)
        s_ref[h] = s * jnp.exp(g_last) + _dot_tn(k_tail, v_new, prec)
        o = o * lax.rsqrt(jnp.mean(o * o, axis=-1, keepdims=True) + RMS_EPS)
        oa_ref[:, sl] = o * ng_ref[:, sl] * _silu(z[:, sl])

    @pl.when(n == pl.num_programs(1) - 1)
    def _():
        dn_ref[...] = s_ref[...]


def _delta(ha, hz, hg, conv0, dn0, w_conv, a_log, dt_bias, norm_g, full_f32):
    b, l, _ = ha.shape
    c = min(CHUNK, l)
    row = lambda w: pl.BlockSpec((None, c, w), lambda i, n: (i, n, 0))
    whole = lambda shape: pl.BlockSpec(shape, lambda i, n: (0,) * len(shape))
    state = pl.BlockSpec((None, N_HEADS, HEAD_DIM, HEAD_DIM), lambda i, n: (i, 0, 0, 0))
    return pl.pallas_call(
        functools.partial(_delta_kernel, c, full_f32),
        grid=(b, l // c),
        in_specs=[row(QKV_WIDTH), row(HEADS_WIDTH), row(GATE_WIDTH),
                  pl.BlockSpec((None, DN_CONV - 1, QKV_WIDTH), lambda i, n: (i, 0, 0)), state,
                  whole((DN_CONV, QKV_WIDTH)), whole((1, GATE_WIDTH)), whole((1, GATE_WIDTH)),
                  whole((1, HEADS_WIDTH))],
        out_specs=[row(HEADS_WIDTH), state],
        out_shape=[jax.ShapeDtypeStruct((b, l, HEADS_WIDTH), F32),
                   jax.ShapeDtypeStruct((b, N_HEADS, HEAD_DIM, HEAD_DIM), F32)],
        scratch_shapes=[pltpu.VMEM((N_HEADS, HEAD_DIM, HEAD_DIM), F32),
                        pltpu.VMEM((8 + c, QKV_WIDTH), F32)],
        compiler_params=_params("arbitrary", "arbitrary"),
        name="delta_rule",
    )(ha, hz, hg, conv0, dn0, w_conv, a_log, dt_bias, norm_g)


PAIR_ROWS = 2 * CHUNK
N_PAIRS = N_HEADS // 2


def _attn_kernel(c, tq, mask_before_start, full_f32, q_ref, kp_ref, ks_ref, vp_ref, vs_ref, bias_ref, o_ref,
                 kwin_ref, vwin_ref):
    i = pl.program_id(1)
    prec = HIGHEST if full_f32 else None
    w = BAND_ROWS
    kwin_ref[0:w, :] = kp_ref[...]
    kwin_ref[w:w + tq, :] = ks_ref[...]
    vwin_ref[0:w, :] = vp_ref[...]
    vwin_ref[w:w + tq, :] = vs_ref[...]
    key_idx = lax.broadcasted_iota(jnp.int32, (1, w + c), 1)

    def chunk(j, carry):
        r0 = pl.multiple_of(j * c, c)
        q = q_ref[pl.ds(r0, c), :]
        kk = kwin_ref[pl.ds(r0, w + c), :]
        vv = vwin_ref[pl.ds(r0, w + c), :]
        valid = (i * tq + r0 + key_idx - w) >= 0
        for h in range(N_HEADS):
            sl = slice(h * HEAD_DIM, (h + 1) * HEAD_DIM)
            s = _dot_nt(q[:, sl], kk[:, sl], prec) * HEAD_DIM ** -0.5 + bias_ref[h]
            if mask_before_start:
                s = jnp.where(valid, s, -1e30)
            s = s - jnp.max(s, axis=-1, keepdims=True)
            p = jnp.exp(s)
            p = p / jnp.sum(p, axis=-1, keepdims=True)
            o_ref[pl.ds(r0, c), sl] = _dot(p, vv[:, sl], prec)
        return carry

    lax.fori_loop(0, tq // c, chunk, 0)


def _attention(q, k, v, k_hist, v_hist, bias, full_f32):
    b, l, _ = q.shape
    c = min(CHUNK, l)
    tq = min(ROW_TILE, l)
    assert tq == BAND_ROWS or l == tq
    own = pl.BlockSpec((None, tq, HEADS_WIDTH), lambda bi, i: (bi, i, 0))
    if k_hist is None:
        prev = pl.BlockSpec((None, BAND_ROWS, HEADS_WIDTH), lambda bi, i: (bi, jnp.maximum(i - 1, 0), 0))
        k_prev, v_prev = k, v
    else:
        prev = pl.BlockSpec((None, BAND_ROWS, HEADS_WIDTH), lambda bi, i: (bi, 0, 0))
        k_prev, v_prev = k_hist, v_hist
    return pl.pallas_call(
        functools.partial(_attn_kernel, c, tq, k_hist is None, full_f32),
        grid=(b, l // tq),
        in_specs=[own, prev, own, prev, own,
                  pl.BlockSpec((N_HEADS, c, BAND_ROWS + c), lambda bi, i: (0, 0, 0))],
        out_specs=own,
        out_shape=jax.ShapeDtypeStruct((b, l, HEADS_WIDTH), F32),
        scratch_shapes=[pltpu.VMEM((BAND_ROWS + tq, HEADS_WIDTH), F32),
                        pltpu.VMEM((BAND_ROWS + tq, HEADS_WIDTH), F32)],
        compiler_params=_params("arbitrary", "arbitrary"),
        name="band_attention",
    )(q, k_prev, k, v_prev, v, bias)


STAGED_ROWS = 256
STAGED_CHUNKS = STAGED_ROWS // CHUNK
STAGED_PROBLEMS = STAGED_CHUNKS * N_PAIRS
STAGED_BATCH = 2


def _delta_staged_kernel(qn_ref, kn_ref, v_ref, hz_ref, hg_ref, dn0_ref, alog_ref, dtb_ref, ng_ref,
                         tril_ref, pones_ref,
                         oa_ref, dn_ref,
                         z_ref, gc_ref, beta_ref, gct_ref,
                         pw_ref, sol_ref, a2_ref, ob_ref, qe_ref, kmat_ref, bsw_ref, egl_ref):
    n = pl.program_id(1)
    c, rows = CHUNK, STAGED_ROWS
    lane = lax.broadcasted_iota(jnp.int32, (1, LANES), 1)
    lo = lane < HEAD_DIM
    zero64 = jnp.zeros((HEAD_DIM, HEAD_DIM), F32)

    nb = STAGED_BATCH
    blocks = rows // LANES

    @pl.when(n == 0)
    def _():
        for bb in range(nb):
            for p in range(N_PAIRS):
                top = jnp.concatenate([zero64, dn0_ref[bb, 2 * p + 1]], axis=1)
                bot = jnp.concatenate([dn0_ref[bb, 2 * p], zero64], axis=1)
                z_ref[bb * N_PAIRS + p] = jnp.concatenate([top, bot], axis=0)

    neg_rate = jnp.where(lane < N_HEADS, -jnp.exp(alog_ref[...]), 0.0)
    for bb in range(nb):
        hg = hg_ref[bb]
        gc = sum(_dot(tril_ref[...], piece) for piece in _split3(neg_rate * jax.nn.softplus(hg + dtb_ref[...])))
        gc_ref[bb] = gc
        beta_ref[bb] = jax.nn.sigmoid(hg)
        for blk in range(blocks):
            gct_ref[bb * blocks + blk] = gc[blk * LANES:(blk + 1) * LANES, :].T[0:8, :]

    row_i = lax.broadcasted_iota(jnp.int32, (c, 1), 0)
    col_j = lane % HEAD_DIM
    incl = row_i >= col_j
    strict = row_i > col_j
    pr = lax.broadcasted_iota(jnp.int32, (LANES, LANES), 0) // HEAD_DIM
    pc = lax.broadcasted_iota(jnp.int32, (LANES, LANES), 1) // HEAD_DIM
    anti = pr != pc
    swap = lambda x: jnp.concatenate([x[HEAD_DIM:], x[:HEAD_DIM]], axis=0)
    halves = lambda x: jnp.concatenate([jnp.where(lo, x, jnp.zeros_like(x)), jnp.where(lo, jnp.zeros_like(x), x)],
                                       axis=0)
    problems = [(bb, s, p) for bb in range(nb) for s in range(STAGED_CHUNKS) for p in range(N_PAIRS)]

    def gate_cols(bb, s, p):
        rs = slice(s * c, (s + 1) * c)
        e, o = 2 * p, 2 * p + 1
        gcs = gc_ref[bb, rs, :]
        g_e, g_o = gcs[:, e:e + 1], gcs[:, o:o + 1]
        return g_e, g_o, g_e[c - 1:c, :], g_o[c - 1:c, :]

    for gi, (bb, s, p) in enumerate(problems):
        rs = slice(s * c, (s + 1) * c)
        ls = slice(p * LANES, (p + 1) * LANES)
        e, o = 2 * p, 2 * p + 1
        kp, vp, qp = kn_ref[bb, rs, ls], v_ref[bb, rs, ls], qn_ref[bb, rs, ls]
        ksw = pltpu.roll(kp, HEAD_DIM, axis=1)
        g_e, g_o, _, _ = gate_cols(bb, s, p)
        betas = beta_ref[bb, rs, :]
        b_e, b_o = betas[:, N_HEADS + e:N_HEADS + e + 1], betas[:, N_HEADS + o:N_HEADS + o + 1]
        gt = gct_ref[bb * blocks + s // 2]
        gt_sw = pltpu.roll(gt, HEAD_DIM, axis=1)
        if s % 2 == 0:
            g_row = jnp.where(lo, gt[e:e + 1, :], gt_sw[o:o + 1, :])
        else:
            g_row = jnp.where(lo, gt_sw[e:e + 1, :], gt[o:o + 1, :])
        diff = jnp.where(lo, g_e, g_o) - g_row
        decay = jnp.where(incl, jnp.exp(jnp.where(incl, diff, 0.0)), 0.0)
        kb = kp * jnp.where(lo, b_e, b_o)
        st = _dot_nt(jnp.concatenate([kb, qp], axis=0).astype(BF16), halves(kp.astype(BF16)))
        pw_ref[gi] = jnp.where(strict, -(st[0:c] * decay), 0.0).astype(BF16)
        a2_ref[gi] = jnp.where(incl, st[c:2 * c] * decay, 0.0).astype(BF16)
        sol_ref[gi, :, 0:LANES] = jnp.where(lo, vp, ksw * jnp.exp(g_e)) * b_e
        sol_ref[gi, :, LANES:2 * LANES] = jnp.where(lo, ksw * jnp.exp(g_o), vp) * b_o

    zeros_cl = jnp.zeros((c, LANES), BF16)
    n_stage = int(math.log2(c))
    for stage in range(n_stage):
        tail = stage == n_stage - 1
        for gi in range(len(problems)):
            power = pw_ref[gi]
            sol = sol_ref[gi]
            sb = sol.astype(BF16)
            top = [sb[:, 0:LANES], zeros_cl] + ([] if tail else [jnp.where(lo, power, zeros_cl)])
            bot = [zeros_cl, sb[:, LANES:2 * LANES]] + ([] if tail else [jnp.where(lo, zeros_cl, power)])
            rhs = jnp.concatenate([jnp.concatenate(top, axis=1), jnp.concatenate(bot, axis=1)], axis=0)
            res = _dot(power, rhs)
            sol_ref[gi] = sol + res[:, 0:2 * LANES]
            if not tail:
                pw_ref[gi] = res[:, 2 * LANES:3 * LANES].astype(BF16)

    for gi, (bb, s, p) in enumerate(problems):
        rs = slice(s * c, (s + 1) * c)
        ls = slice(p * LANES, (p + 1) * LANES)
        sol = sol_ref[gi]
        sol_e, sol_o = sol[:, 0:LANES], sol[:, LANES:2 * LANES]
        g_e, g_o, gl_e, gl_o = gate_cols(bb, s, p)
        top = jnp.concatenate([jnp.where(lo, sol_e, 0.0), jnp.where(lo, 0.0, sol_e)], axis=1)
        bot = jnp.concatenate([jnp.where(lo, 0.0, sol_o), jnp.where(lo, sol_o, 0.0)], axis=1)
        y = _dot(a2_ref[gi], jnp.concatenate([top, bot], axis=0).astype(BF16))
        ob_ref[gi] = y[:, 0:LANES]
        qsw = pltpu.roll(qn_ref[bb, rs, ls], HEAD_DIM, axis=1)
        qe_ref[gi] = (qsw * jnp.where(lo, jnp.exp(g_o), jnp.exp(g_e)) - y[:, LANES:2 * LANES]).astype(BF16)
        kt = kn_ref[bb, rs, ls] * jnp.exp(jnp.where(lo, gl_e - g_e, gl_o - g_o))
        kbm = _dot_tn(halves(kt.astype(BF16)), jnp.concatenate([sol_e, sol_o], axis=0).astype(BF16))
        kmat_ref[gi] = jnp.where(anti, kbm, 0.0).astype(BF16)
        bsw_ref[gi] = swap(jnp.where(anti, 0.0, kbm))
        egl_ref[gi] = jnp.exp(jnp.where(lo, gl_e, gl_o))

    for s in range(STAGED_CHUNKS):
        rs = slice(s * c, (s + 1) * c)
        for bb in range(nb):
            for p in range(N_PAIRS):
                gi = (bb * STAGED_CHUNKS + s) * N_PAIRS + p
                zi = bb * N_PAIRS + p
                ls = slice(p * LANES, (p + 1) * LANES)
                zst = z_ref[zi]
                zb = zst.astype(BF16)
                out = ob_ref[gi] + _dot(qe_ref[gi], zb)
                z_ref[zi] = zst * egl_ref[gi] - swap(_dot(kmat_ref[gi], zb)) + bsw_ref[gi]
                ms = _dot((out * out).astype(BF16), pones_ref[...]) * (1.0 / HEAD_DIM)
                oa_ref[bb, rs, ls] = out * lax.rsqrt(ms + RMS_EPS) * ng_ref[:, ls] * _silu(hz_ref[bb, rs, ls])

    @pl.when(n == pl.num_programs(1) - 1)
    def _():
        for bb in range(nb):
            for p in range(N_PAIRS):
                zst = z_ref[bb * N_PAIRS + p]
                dn_ref[bb, 2 * p] = zst[HEAD_DIM:, 0:HEAD_DIM]
                dn_ref[bb, 2 * p + 1] = zst[0:HEAD_DIM, HEAD_DIM:]


def _split3(x):
    a = x.astype(BF16)
    r = x - a.astype(F32)
    b = r.astype(BF16)
    return a, b, (r - b.astype(F32)).astype(BF16)


def _head_ones():
    hi, hj = jnp.arange(HEADS_WIDTH)[:, None] // HEAD_DIM, jnp.arange(HEADS_WIDTH)[None, :] // HEAD_DIM
    return (hi == hj).astype(BF16)


def _delta_staged(qn, kn, v, hz, hg, dn0, a_log, dt_bias, norm_g):
    b, l, _ = qn.shape
    rows, c, nb = STAGED_ROWS, CHUNK, STAGED_BATCH
    g = nb * STAGED_PROBLEMS
    assert b % nb == 0 and l % rows == 0
    ri, ci = jnp.arange(rows)[:, None], jnp.arange(rows)[None, :]
    tril = ((ri >= ci) & (ri // c == ci // c)).astype(BF16)
    pair_ones = _head_ones()[0:LANES, 0:LANES]
    row = lambda w: pl.BlockSpec((nb, rows, w), lambda i, n: (i, n, 0))
    whole = lambda shape: pl.BlockSpec(shape, lambda i, n: (0,) * len(shape))
    state = pl.BlockSpec((nb, N_HEADS, HEAD_DIM, HEAD_DIM), lambda i, n: (i, 0, 0, 0))
    return pl.pallas_call(
        _delta_staged_kernel,
        grid=(b // nb, l // rows),
        in_specs=[row(HEADS_WIDTH), row(HEADS_WIDTH), row(HEADS_WIDTH), row(HEADS_WIDTH), row(GATE_WIDTH), state,
                  whole((1, GATE_WIDTH)), whole((1, GATE_WIDTH)),
                  whole((1, HEADS_WIDTH)), whole((rows, rows)), whole((LANES, LANES))],
        out_specs=[row(HEADS_WIDTH), state],
        out_shape=[jax.ShapeDtypeStruct((b, l, HEADS_WIDTH), F32),
                   jax.ShapeDtypeStruct((b, N_HEADS, HEAD_DIM, HEAD_DIM), F32)],
        scratch_shapes=[pltpu.VMEM((nb * N_PAIRS, LANES, LANES), F32),
                        pltpu.VMEM((nb, rows, GATE_WIDTH), F32),
                        pltpu.VMEM((nb, rows, GATE_WIDTH), F32),
                        pltpu.VMEM((nb * rows // LANES, 8, LANES), F32),
                        pltpu.VMEM((g, c, LANES), BF16),
                        pltpu.VMEM((g, c, 2 * LANES), F32),
                        pltpu.VMEM((g, c, LANES), BF16),
                        pltpu.VMEM((g, c, LANES), F32),
                        pltpu.VMEM((g, c, LANES), BF16),
                        pltpu.VMEM((g, LANES, LANES), BF16),
                        pltpu.VMEM((g, LANES, LANES), F32),
                        pltpu.VMEM((g, 1, LANES), F32)],
        compiler_params=_params("arbitrary", "arbitrary"),
        name="delta_rule_staged",
    )(qn, kn, v, hz, hg, dn0, a_log, dt_bias, norm_g, tril, pair_ones)


ATTN_WINDOW = BAND_ROWS + PAIR_ROWS


def _attn_pairs_kernel(q_ref, kp_ref, ks_ref, vp_ref, vs_ref, bias_ref, o_ref, kwin_ref, vwin_ref):
    i = pl.program_id(1)
    w, tq = BAND_ROWS, ROW_TILE
    kwin_ref[0:w, :] = kp_ref[...]
    kwin_ref[w:w + tq, :] = ks_ref[...]
    vwin_ref[0:w, :] = vp_ref[...]
    vwin_ref[w:w + tq, :] = vs_ref[...]
    lo = lax.broadcasted_iota(jnp.int32, (1, LANES), 1) < HEAD_DIM
    key_idx = lax.broadcasted_iota(jnp.int32, (1, ATTN_WINDOW), 1)
    ones = jnp.ones((ATTN_WINDOW, LANES), BF16)

    def chunk_pair(mask_start, m, carry):
        r0 = pl.multiple_of(m * PAIR_ROWS, PAIR_ROWS)
        for p in range(N_PAIRS):
            ls = slice(p * LANES, (p + 1) * LANES)
            q = q_ref[pl.ds(r0, PAIR_ROWS), ls] * HEAD_DIM ** -0.5
            kk = kwin_ref[pl.ds(r0, ATTN_WINDOW), ls]
            vv = vwin_ref[pl.ds(r0, ATTN_WINDOW), ls]
            zero = jnp.zeros_like(q)
            q2 = jnp.concatenate([jnp.where(lo, q, zero), jnp.where(lo, zero, q)], axis=0)
            s = _dot_nt(q2, kk) + bias_ref[p]
            if mask_start:
                s = jnp.where(r0 + key_idx < w, -1e30, s)
            pexp = jnp.exp(s - jnp.max(s, axis=-1, keepdims=True)).astype(BF16)
            pv = _dot(pexp, jnp.concatenate([vv, ones], axis=1))
            pv = pv[:, 0:LANES] / pv[:, LANES:2 * LANES]
            o_ref[pl.ds(r0, PAIR_ROWS), ls] = jnp.where(lo, pv[0:PAIR_ROWS], pv[PAIR_ROWS:2 * PAIR_ROWS])
        return carry

    @pl.when(i == 0)
    def _():
        lax.fori_loop(0, tq // PAIR_ROWS, functools.partial(chunk_pair, True), 0, unroll=4)

    @pl.when(i > 0)
    def _():
        lax.fori_loop(0, tq // PAIR_ROWS, functools.partial(chunk_pair, False), 0, unroll=4)


def _pair_bias(rel_table):
    a = jnp.arange(PAIR_ROWS)[:, None]
    idx = jnp.arange(ATTN_WINDOW)[None, :]
    first = (a // CHUNK) * CHUNK
    in_band = (idx >= first) & (idx < first + BAND_ROWS + CHUNK)
    bias = jnp.where(in_band[None], _rel_bias(rel_table, PAIR_ROWS), -1e30)
    return bias.reshape(N_PAIRS, 2 * PAIR_ROWS, ATTN_WINDOW)


def _attention_pairs(qkv, bias):
    b, l, _ = qkv.shape
    tq = ROW_TILE
    assert tq == BAND_ROWS and l % tq == 0
    own = lambda j: pl.BlockSpec((None, tq, HEADS_WIDTH), lambda bi, i: (bi, i, j))
    prev = lambda j: pl.BlockSpec((None, BAND_ROWS, HEADS_WIDTH), lambda bi, i: (bi, jnp.maximum(i - 1, 0), j))
    return pl.pallas_call(
        _attn_pairs_kernel,
        grid=(b, l // tq),
        in_specs=[own(0), prev(1), own(1), prev(2), own(2),
                  pl.BlockSpec((N_PAIRS, 2 * PAIR_ROWS, ATTN_WINDOW), lambda bi, i: (0, 0, 0))],
        out_specs=own(0),
        out_shape=jax.ShapeDtypeStruct((b, l, HEADS_WIDTH), F32),
        scratch_shapes=[pltpu.VMEM((BAND_ROWS + tq, HEADS_WIDTH), BF16),
                        pltpu.VMEM((BAND_ROWS + tq, HEADS_WIDTH), BF16)],
        compiler_params=_params("arbitrary", "arbitrary"),
        name="band_attention_pairs",
    )(qkv, qkv, qkv, qkv, qkv, bias)


def _mix_kernel(tile, start, full_f32, oa_ref, ob_ref, hc_ref, x_ref, pool0_ref, pw_ref, ps_ref, wout_ref, g_ref, b_ref,
                x1_ref, ext_ref):
    i = pl.program_id(1)
    prec = HIGHEST if full_f32 else None
    wdt = wout_ref.dtype
    hist0 = 16 - POOL_HIST

    @pl.when(i == 0)
    def _():
        ext_ref[0:hist0, :] = jnp.zeros((hist0, POOL_WIDTH), F32)
        ext_ref[hist0:16, :] = pool0_ref[...]

    u = hc_ref[...]
    ext_ref[16:16 + tile, :] = u
    window = 2 << (lax.broadcasted_iota(jnp.int32, (1, POOL_WIDTH), 1) // POOL_GROUP_DIM)
    level = ext_ref[...]
    sums = {}
    for wdw in POOL_WINDOWS:
        level = level + pltpu.roll(level, wdw // 2, axis=0)
        sums[wdw] = level[16:16 + tile, :]
    wsum = sums[POOL_WINDOWS[-1]]
    for wdw in reversed(POOL_WINDOWS[:-1]):
        wsum = jnp.where(window == wdw, sums[wdw], wsum)
    ext_ref[hist0:16, :] = u[tile - POOL_HIST:tile, :]
    pos1 = start + i * tile + 1 + lax.broadcasted_iota(jnp.int32, (tile, 1), 0)
    cnt = jnp.minimum(pos1, window).astype(F32)
    res = wsum / cnt - u
    oc = _dot(res, pw_ref[...], prec) * ps_ref[...]

    mixed = jnp.concatenate([oa_ref[...].astype(wdt), ob_ref[...].astype(wdt), oc.astype(wdt)], axis=1)
    mix = _dot(mixed, wout_ref[...], prec)
    x1_ref[...] = _layer_norm(DEEPNORM_ALPHA * x_ref[...] + mix, g_ref[...], b_ref[...])


def _mix(oa, ob, hc, x, pool0, pool_wbd, pool_scale, w_out, ln_g, ln_b, start, full_f32):
    b, l, _ = x.shape
    tile = min(ROW_TILE, l)
    row = lambda w: pl.BlockSpec((None, tile, w), lambda bi, i: (bi, i, 0))
    whole = lambda shape: pl.BlockSpec(shape, lambda bi, i: (0,) * len(shape))
    return pl.pallas_call(
        functools.partial(_mix_kernel, tile, start, full_f32),
        grid=(b, l // tile),
        in_specs=[row(HEADS_WIDTH), row(HEADS_WIDTH), row(POOL_WIDTH), row(D_MODEL),
                  pl.BlockSpec((None, POOL_HIST, POOL_WIDTH), lambda bi, i: (bi, 0, 0)),
                  whole((POOL_WIDTH, POOL_WIDTH)), whole((1, POOL_WIDTH)), whole((D_MODEL, D_MODEL)),
                  whole((1, D_MODEL)), whole((1, D_MODEL))],
        out_specs=row(D_MODEL),
        out_shape=jax.ShapeDtypeStruct((b, l, D_MODEL), F32),
        scratch_shapes=[pltpu.VMEM((16 + tile, POOL_WIDTH), F32)],
        compiler_params=_params("arbitrary", "arbitrary"),
        name="pool_outproj_ln1",
    )(oa, ob, hc, x, pool0, pool_wbd, pool_scale, w_out, ln_g, ln_b)


def _top2_of4(a, b, c, d):
    hi1, lo1 = jnp.maximum(a, b), jnp.minimum(a, b)
    hi2, lo2 = jnp.maximum(c, d), jnp.minimum(c, d)
    return jnp.maximum(hi1, hi2), jnp.maximum(jnp.minimum(hi1, hi2), jnp.maximum(lo1, lo2))


def _route(logits_t, bias_t):
    aff = jax.nn.sigmoid(logits_t[0:N_EXPERTS, :])
    sel = aff + bias_t[0:N_EXPERTS, :]
    t = sel.shape[1]
    scores = []
    for gi in range(N_EXPERT_GROUPS):
        r = [sel[4 * gi + m:4 * gi + m + 1, :] for m in range(EXPERTS_PER_GROUP)]
        top1, top2 = _top2_of4(*r)
        scores.append(top1 + top2)
    best = scores[0]
    best_g = jnp.zeros((1, t), jnp.int32)
    for gi in range(1, N_EXPERT_GROUPS):
        better = scores[gi] > best
        best = jnp.where(better, scores[gi], best)
        best_g = jnp.where(better, gi, best_g)
    e_idx = lax.broadcasted_iota(jnp.int32, (N_EXPERTS, t), 0)
    masked = jnp.where(e_idx // EXPERTS_PER_GROUP == best_g, sel, -jnp.inf)
    m1 = jnp.max(masked, axis=0, keepdims=True)
    i1 = jnp.min(jnp.where(masked == m1, e_idx, N_EXPERTS), axis=0, keepdims=True)
    rest = jnp.where(e_idx == i1, -jnp.inf, masked)
    m2 = jnp.max(rest, axis=0, keepdims=True)
    i2 = jnp.min(jnp.where(rest == m2, e_idx, N_EXPERTS), axis=0, keepdims=True)
    w1 = jnp.sum(jnp.where(e_idx == i1, aff, 0.0), axis=0, keepdims=True)
    w2 = jnp.sum(jnp.where(e_idx == i2, aff, 0.0), axis=0, keepdims=True)
    tot = w1 + w2
    gates = jnp.where(e_idx == i1, w1 / tot, 0.0) + jnp.where(e_idx == i2, w2 / tot, 0.0)
    return jnp.concatenate([gates, jnp.zeros((LANES - N_EXPERTS, t), F32)], axis=0), best_g


def _expert(xb, w_gu, w_d, gates, e, prec):
    gu = _dot(xb, w_gu, prec)
    hid = _silu(gu[:, 0:D_EXPERT]) * gu[:, D_EXPERT:2 * D_EXPERT]
    lane = lax.broadcasted_iota(jnp.int32, (1, LANES), 1)
    ge = jnp.sum(jnp.where(lane == e, gates, 0.0), axis=-1, keepdims=True)
    return _dot(hid.astype(w_d.dtype), w_d, prec) * ge


def _ffn_kernel(x_ref, wr_ref, rb_ref, wgu_ref, wd_ref, g_ref, b_ref, y_ref):
    x = x_ref[...]
    xb = x.astype(BF16)
    logits = _dot(xb, wr_ref[...])
    gates = _route(logits.T, rb_ref[...])[0].T
    acc = jnp.zeros(x.shape, F32)
    for e in range(N_EXPERTS):
        acc = acc + _expert(xb, wgu_ref[e], wd_ref[e], gates, e, None)
    y_ref[...] = _layer_norm(DEEPNORM_ALPHA * x + acc, g_ref[...], b_ref[...])


def _ffn_f32_kernel(x_ref, wr_ref, rb_ref, wgu_ref, wd_ref, g_ref, b_ref, y_ref, gates_ref, acc_ref):
    e = pl.program_id(1)
    x = x_ref[...]

    @pl.when(e == 0)
    def _():
        logits = _dot(x, wr_ref[...], HIGHEST)
        gates_ref[...] = _route(logits.T, rb_ref[...])[0].T
        acc_ref[...] = jnp.zeros(acc_ref.shape, F32)

    acc_ref[...] += _expert(x, wgu_ref[...], wd_ref[...], gates_ref[...], e, HIGHEST)

    @pl.when(e == N_EXPERTS - 1)
    def _():
        y_ref[...] = _layer_norm(DEEPNORM_ALPHA * x + acc_ref[...], g_ref[...], b_ref[...])


def _ffn(x2d, w_router, router_bias, w_gu, w_down, ln_g, ln_b, full_f32):
    t = x2d.shape[0]
    tm = min(ROW_TILE, t)
    out_shape = jax.ShapeDtypeStruct((t, D_MODEL), F32)
    if full_f32:
        whole = lambda shape: pl.BlockSpec(shape, lambda i, e: (0,) * len(shape))
        return pl.pallas_call(
            _ffn_f32_kernel,
            grid=(t // tm, N_EXPERTS),
            in_specs=[pl.BlockSpec((tm, D_MODEL), lambda i, e: (i, 0)),
                      whole((D_MODEL, LANES)), whole((LANES, 1)),
                      pl.BlockSpec((None, D_MODEL, 2 * D_EXPERT), lambda i, e: (e, 0, 0)),
                      pl.BlockSpec((None, D_EXPERT, D_MODEL), lambda i, e: (e, 0, 0)),
                      whole((1, D_MODEL)), whole((1, D_MODEL))],
            out_specs=pl.BlockSpec((tm, D_MODEL), lambda i, e: (i, 0)),
            out_shape=out_shape,
            scratch_shapes=[pltpu.VMEM((tm, LANES), F32), pltpu.VMEM((tm, D_MODEL), F32)],
            compiler_params=_params("arbitrary", "arbitrary"),
            name="routed_ffn_ln2_f32",
        )(x2d, w_router, router_bias, w_gu, w_down, ln_g, ln_b)
    whole = lambda shape: pl.BlockSpec(shape, lambda i: (0,) * len(shape), pipeline_mode=pl.Buffered(1))
    return pl.pallas_call(
        _ffn_kernel,
        grid=(t // tm,),
        in_specs=[pl.BlockSpec((tm, D_MODEL), lambda i: (i, 0)),
                  whole((D_MODEL, LANES)), whole((LANES, 1)),
                  whole((N_EXPERTS, D_MODEL, 2 * D_EXPERT)), whole((N_EXPERTS, D_EXPERT, D_MODEL)),
                  whole((1, D_MODEL)), whole((1, D_MODEL))],
        out_specs=pl.BlockSpec((tm, D_MODEL), lambda i: (i, 0)),
        out_shape=out_shape,
        compiler_params=_params("arbitrary"),
        name="routed_ffn_ln2",
    )(x2d, w_router, router_bias, w_gu, w_down, ln_g, ln_b)


def _prep_layer(l, wdt, w_in, w_conv, a_log, dt_bias, dn_norm_g, rel_table, pool_w, pool_scale, w_out,
                ln1_g, ln1_b, w_gate, w_up, w_down, ln2_g, ln2_b):
    wi = w_in[l]
    gates = jnp.zeros((D_MODEL, GATE_WIDTH), F32).at[:, 0:2 * N_HEADS].set(wi[:, OFF_AA:OFF_BQ])
    w_perm = jnp.concatenate([wi[:, OFF_AQ:OFF_AZ], wi[:, OFF_AZ:OFF_AA], wi[:, OFF_BQ:OFF_CU],
                              wi[:, OFF_CU:IN_WIDTH], gates], axis=1).astype(wdt)
    pad_heads = lambda vec: jnp.zeros((1, GATE_WIDTH), F32).at[0, 0:N_HEADS].set(vec)
    pool_wbd = jnp.zeros((POOL_WIDTH, POOL_WIDTH), F32)
    for gi in range(len(POOL_WINDOWS)):
        s = slice(gi * POOL_GROUP_DIM, (gi + 1) * POOL_GROUP_DIM)
        pool_wbd = pool_wbd.at[s, s].set(pool_w[l, gi])
    return dict(
        w_perm=w_perm, w_conv=w_conv[l], a_log=pad_heads(a_log[l]), dt_bias=pad_heads(dt_bias[l]),
        norm_g=jnp.tile(dn_norm_g[l], N_HEADS)[None, :], rel_table=rel_table[l],
        pool_wbd=pool_wbd, pool_scale=pool_scale[l][None, :], w_out=w_out[l].astype(wdt),
        ln1_g=ln1_g[l][None, :], ln1_b=ln1_b[l][None, :],
        w_gu=jnp.concatenate([w_gate[l], w_up[l]], axis=-1).astype(wdt), w_down=w_down[l].astype(wdt),
        ln2_g=ln2_g[l][None, :], ln2_b=ln2_b[l][None, :])


def _rel_bias(rel_table, c):
    w = BAND_ROWS + c
    period = c + w
    m = jnp.arange(period)
    delta = jnp.where(m < w, -m, period - m)
    vec = rel_table[:, jnp.clip(delta + BAND_ROWS, -REL_CLIP, REL_CLIP) + REL_CLIP].astype(F32)
    rolled = jnp.tile(vec, (1, c))[:, :c * (period - 1)].reshape(rel_table.shape[0], c, period - 1)
    return rolled[:, :, :w]


def _last_rows(t, n):
    if t.shape[1] < n:
        t = jnp.concatenate([jnp.zeros((t.shape[0], n - t.shape[1]) + t.shape[2:], t.dtype), t], axis=1)
    return t[:, t.shape[1] - n:]


def _trunk(x, start, dn_state, conv_state, k_hist, v_hist, pool_state, band_rows, layers_f32, layers_bf16, w_router,
           router_bias):
    b, l, _ = x.shape
    assert l >= POOL_HIST
    c = min(CHUNK, l)
    new_dn, new_conv, new_k, new_v, new_pool = [], [], [], [], []
    for li, p32 in enumerate(layers_f32):
        full_f32 = li < len(layers_f32) - 1
        p = p32 if full_f32 else layers_bf16[li]
        ha, hz, hb, hc, hg = _inproj(x.reshape(b * l, D_MODEL), p32["w_perm"], True)
        ha = ha.reshape(b, l, QKV_WIDTH)
        hz = hz.reshape(b, l, HEADS_WIDTH)
        hb = hb.reshape(b, l, QKV_WIDTH)
        hc = hc.reshape(b, l, POOL_WIDTH)
        hg = hg.reshape(b, l, GATE_WIDTH)
        oa, dn_new = _delta(ha, hz, hg, conv_state[li], dn_state[li], p["w_conv"], p["a_log"], p["dt_bias"],
                            p["norm_g"], full_f32)
        qb, kb, vb = (hb[..., j * HEADS_WIDTH:(j + 1) * HEADS_WIDTH] for j in range(3))
        bias = _rel_bias(p["rel_table"], c)
        if k_hist is None:
            ob = _attention(qb, kb, vb, None, None, bias, full_f32)
            k_new, v_new = _last_rows(kb, band_rows), _last_rows(vb, band_rows)
        else:
            kh = k_hist[li].reshape(b, -1, HEADS_WIDTH)
            vh = v_hist[li].reshape(b, -1, HEADS_WIDTH)
            ob = _attention(qb, kb, vb, kh, vh, bias, full_f32)
            k_new = _last_rows(jnp.concatenate([kh, kb], axis=1), band_rows)
            v_new = _last_rows(jnp.concatenate([vh, vb], axis=1), band_rows)
        x1 = _mix(oa, ob, hc, x, pool_state[li], p["pool_wbd"], p["pool_scale"], p["w_out"], p["ln1_g"],
                  p["ln1_b"], start, full_f32)
        x = _ffn(x1.reshape(b * l, D_MODEL), w_router.astype(p["w_out"].dtype), router_bias, p["w_gu"],
                 p["w_down"], p["ln2_g"], p["ln2_b"], full_f32).reshape(b, l, D_MODEL)
        new_dn.append(dn_new)
        new_conv.append(ha[:, l - (DN_CONV - 1):])
        new_k.append(k_new.reshape(b, band_rows, N_HEADS, HEAD_DIM))
        new_v.append(v_new.reshape(b, band_rows, N_HEADS, HEAD_DIM))
        new_pool.append(hc[:, l - POOL_HIST:])
    return (x, jnp.stack(new_dn), jnp.stack(new_conv), jnp.stack(new_k), jnp.stack(new_v), jnp.stack(new_pool))


def _trunk_seq(x, dn_state, conv_state, pool_state, band_rows, layers, w_router, router_bias):
    b, l, _ = x.shape
    assert band_rows == BAND_ROWS and l >= band_rows
    new_dn, new_conv, new_k, new_v, new_pool = [], [], [], [], []
    for li, p in enumerate(layers):
        qn, kn, va, hz, hb, hc, hg, kv_last, conv_new = _inproj_seq(x, p["w_perm"], conv_state[li], p["w_conv"])
        oa, dn_new = _delta_staged(qn, kn, va, hz, hg, dn_state[li], p["a_log"], p["dt_bias"], p["norm_g"])
        ob = _attention_pairs(hb, _pair_bias(p["rel_table"]))
        x1 = _mix(oa, ob, hc, x, pool_state[li], p["pool_wbd"], p["pool_scale"], p["w_out"], p["ln1_g"],
                  p["ln1_b"], 0, False)
        x = _ffn(x1.reshape(b * l, D_MODEL), w_router.astype(BF16), router_bias, p["w_gu"], p["w_down"],
                 p["ln2_g"], p["ln2_b"], False).reshape(b, l, D_MODEL)
        new_dn.append(dn_new)
        new_conv.append(conv_new)
        new_k.append(kv_last[..., 0:HEADS_WIDTH].reshape(b, band_rows, N_HEADS, HEAD_DIM))
        new_v.append(kv_last[..., HEADS_WIDTH:].reshape(b, band_rows, N_HEADS, HEAD_DIM))
        new_pool.append(hc[:, l - POOL_HIST:])
    return (x, jnp.stack(new_dn), jnp.stack(new_conv), jnp.stack(new_k), jnp.stack(new_v), jnp.stack(new_pool))


@jax.jit
def kernel(x_prompt, x_sample, state_dn, state_conv, cache_k, cache_v, state_pool, w_in, w_conv, a_log, dt_bias,
           dn_norm_g, rel_table, pool_w, pool_scale, w_out, ln1_g, ln1_b, w_router, router_bias, w_gate, w_up,
           w_down, ln2_g, ln2_b):
    weights = (w_in, w_conv, a_log, dt_bias, dn_norm_g, rel_table, pool_w, pool_scale, w_out,
               ln1_g, ln1_b, w_gate, w_up, w_down, ln2_g, ln2_b)
    layers_bf16 = [_prep_layer(l, BF16, *weights) for l in range(DEPTH)]
    layers_f32 = [_prep_layer(l, F32, *weights) for l in range(DEPTH)]
    w_router_p = jnp.zeros((D_MODEL, LANES), F32).at[:, 0:N_EXPERTS].set(w_router)
    router_bias_p = jnp.zeros((LANES, 1), F32).at[0:N_EXPERTS, 0].set(router_bias)
    band_rows = cache_k.shape[2]
    bp = x_prompt.shape[0]
    zero_dn = jnp.zeros((DEPTH, bp, N_HEADS, HEAD_DIM, HEAD_DIM), state_dn.dtype)
    zero_conv = jnp.zeros((DEPTH, bp, DN_CONV - 1, QKV_WIDTH), x_prompt.dtype)
    zero_pool = jnp.zeros((DEPTH, bp, POOL_HIST, POOL_WIDTH), x_prompt.dtype)
    prompt = _trunk_seq(x_prompt, zero_dn, zero_conv, zero_pool, band_rows, layers_bf16, w_router_p, router_bias_p)
    sample = _trunk(x_sample, PAST_LEN, state_dn, state_conv, cache_k, cache_v, state_pool, band_rows, layers_f32,
                    layers_bf16, w_router_p, router_bias_p)
    return (prompt[0], sample[0]) + prompt[1:] + sample[1:]
```
